```python
import jax
import jax.numpy as jnp
from jax import lax
import numpy as np

D_MODEL = 4096
BATCH = 2
SEQ = 8192
DEPTH = 2

HEAD_DIM = 128
ROPE_THETA = 500000.0
PARTIAL_ROT = HEAD_DIM // 4
NORM_EPS = 1e-6
NEG_INF = -1e30

MLA_HEADS = 16
MLA_Q_RANK = 896
MLA_KV_RANK = 512
MLA_NOPE = 128
MLA_ROPE = 64
MLA_QK = MLA_NOPE + MLA_ROPE
MLA_V = 128
Q_BLOCK = 128

DIL_CONFIGS = ((128, 1), (512, 4), (2048, 16))
DIL_GROUPS = len(DIL_CONFIGS)
DIL_HPG = 8
DIL_HEADS = DIL_GROUPS * DIL_HPG

EVEN_IN = MLA_Q_RANK + MLA_KV_RANK + MLA_ROPE + 3 * DIL_HEADS * HEAD_DIM
EVEN_MIX = MLA_HEADS * MLA_V + DIL_HPG * HEAD_DIM

GRID_W = 64
NA_HEADS = D_MODEL // HEAD_DIM
NA_ROWS = 8
NA_COLS = 16
ODD_MIX = NA_HEADS * HEAD_DIM

D_FF = 11008
CONV_WIDTH = 3

N_EVEN = (DEPTH + 1) // 2
N_ODD = DEPTH // 2

kernel_name = 'hybrid_mla_dilated_natten_convffn_encoder'


def _rmsnorm(x, g):
    xf = x.astype(jnp.float32)
    y = xf * lax.rsqrt(jnp.mean(xf * xf, axis=-1, keepdims=True) + NORM_EPS)
    return (y * g.astype(jnp.float32)).astype(x.dtype)


def _rope(x, pos, rot_dim):
    half = rot_dim // 2
    inv_freq = 1.0 / (ROPE_THETA ** (jnp.arange(half, dtype=jnp.float32) * (2.0 / rot_dim)))
    ang = pos.astype(jnp.float32)[:, None] * inv_freq[None, :]
    cos = jnp.cos(ang)[:, None, :].astype(x.dtype)
    sin = jnp.sin(ang)[:, None, :].astype(x.dtype)
    x1 = x[..., :half]
    x2 = x[..., half:rot_dim]
    return jnp.concatenate([x1 * cos - x2 * sin, x2 * cos + x1 * sin, x[..., rot_dim:]], axis=-1)


def _dense_blocked_attention(q, k, v):
    B, S, H, Dq = q.shape
    nqb = S // Q_BLOCK
    scale = Dq ** -0.5
    qb = q.reshape(B, nqb, Q_BLOCK, H, Dq).transpose(1, 0, 2, 3, 4)

    def block(qi):
        s = jnp.einsum('bqhd,bkhd->bhqk', qi, k).astype(jnp.float32) * scale
        p = jax.nn.softmax(s, axis=-1).astype(v.dtype)
        return jnp.einsum('bhqk,bkhd->bqhd', p, v)

    o = lax.map(block, qb)
    return o.transpose(1, 0, 2, 3, 4).reshape(B, S, H * v.shape[-1])


def _banded(q, k, v, radius):
    N, L, H, Dh = q.shape
    blk = radius
    nb = -(-L // blk)
    Lp = nb * blk
    qb = jnp.pad(q, ((0, 0), (0, Lp - L), (0, 0), (0, 0))).reshape(N, nb, blk, H, Dh)
    kv_pad = ((0, 0), (blk, Lp - L + blk), (0, 0), (0, 0))
    kp = jnp.pad(k, kv_pad).reshape(N, nb + 2, blk, H, Dh)
    vp = jnp.pad(v, kv_pad).reshape(N, nb + 2, blk, H, Dh)
    kb = jnp.concatenate([kp[:, :-2], kp[:, 1:-1], kp[:, 2:]], axis=2)
    vb = jnp.concatenate([vp[:, :-2], vp[:, 1:-1], vp[:, 2:]], axis=2)
    qpos = jnp.arange(nb)[:, None] * blk + jnp.arange(blk)[None, :]
    kpos = jnp.arange(nb)[:, None] * blk - blk + jnp.arange(3 * blk)[None, :]
    valid = ((jnp.abs(qpos[:, :, None] - kpos[:, None, :]) <= radius)
             & (kpos[:, None, :] >= 0) & (kpos[:, None, :] < L))
    s = jnp.einsum('nbqhd,nbkhd->nbhqk', qb, kb).astype(jnp.float32) * (Dh ** -0.5)
    s = jnp.where(valid[None, :, None], s, NEG_INF)
    m = jnp.max(s, axis=-1)
    p = jnp.exp(s - m[..., None])
    l = jnp.sum(p, axis=-1)
    o = jnp.einsum('nbhqk,nbkhd->nbqhd', p.astype(v.dtype), vb)
    m_t = m.transpose(0, 1, 3, 2)
    l_t = l.transpose(0, 1, 3, 2)
    o = (o.astype(jnp.float32) / l_t[..., None]).astype(v.dtype)
    o = o.reshape(N, Lp, H, Dh)[:, :L]
    return o, m_t.reshape(N, Lp, H)[:, :L], l_t.reshape(N, Lp, H)[:, :L]


def _to_strided(t, d):
    B, S, H, Dh = t.shape
    return t.reshape(B, S // d, d, H, Dh).transpose(0, 2, 1, 3, 4).reshape(B * d, S // d, H, Dh)


def _from_strided(t, B, d):
    Ld = t.shape[1]
    t = t.reshape((B, d, Ld) + t.shape[2:])
    t = jnp.swapaxes(t, 1, 2)
    return t.reshape((B, Ld * d) + t.shape[3:])


def _dilated_mixture(q, k, v):
    B, S, _, Dh = q.shape
    outs, ms, ls = [], [], []
    for g, (window, dil) in enumerate(DIL_CONFIGS):
        sl = slice(g * DIL_HPG, (g + 1) * DIL_HPG)
        o, m, l = _banded(_to_strided(q[:, :, sl], dil), _to_strided(k[:, :, sl], dil),
                          _to_strided(v[:, :, sl], dil), window // (2 * dil))
        outs.append(_from_strided(o, B, dil).astype(jnp.float32))
        ms.append(_from_strided(m, B, dil))
        ls.append(_from_strided(l, B, dil))
    m_all = jnp.stack(ms)
    wts = jnp.exp(m_all - jnp.max(m_all, axis=0, keepdims=True)) * jnp.stack(ls)
    o = jnp.sum(wts[..., None] * jnp.stack(outs), axis=0) / jnp.sum(wts, axis=0)[..., None]
    return o.reshape(B, S, DIL_HPG * Dh).astype(q.dtype)


def _neighborhood_attention(q, k, v, rpb):
    B, S, H, Dh = q.shape
    rows = S // GRID_W
    kr = min(NA_ROWS, rows)
    qg = q.reshape(B, rows, GRID_W, H, Dh)
    kg = k.reshape(B, rows, GRID_W, H, Dh)
    vg = v.reshape(B, rows, GRID_W, H, Dh)
    row_start = jnp.clip(jnp.arange(rows) - kr // 2, 0, rows - kr)
    cols = jnp.arange(GRID_W)
    col_idx = jnp.clip(cols - NA_COLS // 2, 0, GRID_W - NA_COLS)[:, None] + jnp.arange(NA_COLS)[None, :]
    col_bias_idx = col_idx - cols[:, None] + (NA_COLS - 1)
    rpb_cols = rpb[:, :, col_bias_idx]
    scale = Dh ** -0.5

    def row(i):
        rs = row_start[i]
        kn = lax.dynamic_slice_in_dim(kg, rs, kr, axis=1)[:, :, col_idx]
        vn = lax.dynamic_slice_in_dim(vg, rs, kr, axis=1)[:, :, col_idx]
        qi = lax.dynamic_index_in_dim(qg, i, axis=1, keepdims=False)
        r_idx = rs + jnp.arange(kr) - i + (NA_ROWS - 1)
        bias = rpb_cols[:, r_idx].transpose(0, 2, 1, 3)
        s = (jnp.einsum('bchd,brckhd->bhcrk', qi, kn).astype(jnp.float32) * scale
             + bias[None].astype(jnp.float32))
        p = jax.nn.softmax(s.reshape(B, H, GRID_W, kr * NA_COLS), axis=-1)
        p = p.reshape(s.shape).astype(v.dtype)
        return jnp.einsum('bhcrk,brckhd->bchd', p, vn)

    o = lax.map(row, jnp.arange(rows))
    return o.transpose(1, 0, 2, 3, 4).reshape(B, S, H * Dh)


def _even_mixer(h, pos, w_in, q_lora_g, kv_lora_g, w_uq, w_ukv, mla_qn, mla_kn, dil_qn, dil_kn, w_out):
    B, S, _ = h.shape
    z = h @ w_in
    c1 = MLA_Q_RANK
    c2 = c1 + MLA_KV_RANK
    c3 = c2 + MLA_ROPE
    c_q, c_kv, k_pe, z_b = z[..., :c1], z[..., c1:c2], z[..., c2:c3], z[..., c3:]
    q = (_rmsnorm(c_q, q_lora_g) @ w_uq).reshape(B, S, MLA_HEADS, MLA_QK)
    kv = (_rmsnorm(c_kv, kv_lora_g) @ w_ukv).reshape(B, S, MLA_HEADS, MLA_NOPE + MLA_V)
    k = jnp.concatenate([kv[..., :MLA_NOPE],
                         jnp.broadcast_to(k_pe[:, :, None, :], (B, S, MLA_HEADS, MLA_ROPE))], axis=-1)
    q = _rmsnorm(q, mla_qn)
    k = _rmsnorm(k, mla_kn)
    q = jnp.concatenate([q[..., :MLA_NOPE], _rope(q[..., MLA_NOPE:], pos, MLA_ROPE)], axis=-1)
    k = jnp.concatenate([k[..., :MLA_NOPE], _rope(k[..., MLA_NOPE:], pos, MLA_ROPE)], axis=-1)
    o_a = _dense_blocked_attention(q, k, kv[..., MLA_NOPE:])
    qkv = z_b.reshape(B, S, 3, DIL_HEADS, HEAD_DIM)
    qb = _rope(_rmsnorm(qkv[:, :, 0], dil_qn), pos, PARTIAL_ROT)
    kb = _rope(_rmsnorm(qkv[:, :, 1], dil_kn), pos, PARTIAL_ROT)
    o_b = _dilated_mixture(qb, kb, qkv[:, :, 2])
    return jnp.concatenate([o_a, o_b], axis=-1) @ w_out


def _odd_mixer(h, w_qkv, qn, kn, rpb, w_out):
    B, S, _ = h.shape
    qkv = (h @ w_qkv).reshape(B, S, 3, NA_HEADS, HEAD_DIM)
    q = _rmsnorm(qkv[:, :, 0], qn)
    k = _rmsnorm(qkv[:, :, 1], kn)
    return _neighborhood_attention(q, k, qkv[:, :, 2], rpb) @ w_out


def _conv_ffn(h, w_gate, w_up, conv_w, conv_b, w_down):
    a = h @ w_gate
    ap = jnp.pad(a, ((0, 0), (1, 1), (0, 0)))
    a = ap[:, :-2] * conv_w[0] + ap[:, 1:-1] * conv_w[1] + ap[:, 2:] * conv_w[2] + conv_b
    return (jax.nn.silu(a) * (h @ w_up)) @ w_down


def setup_inputs(seed: int = 0) -> dict:
    key = jax.random.key(seed)
    ks = iter(jax.random.split(key, 32))
    f32 = jnp.float32

    def w(shape, fan_in):
        return jax.random.normal(next(ks), shape, f32) * (fan_in ** -0.5)

    def gain(shape):
        return 1.0 + 0.02 * jax.random.normal(next(ks), shape, f32)

    return {
        'x': jax.random.normal(next(ks), (BATCH, SEQ, D_MODEL), f32),
        'e_norm': gain((N_EVEN, D_MODEL)),
        'e_w_in': w((N_EVEN, D_MODEL, EVEN_IN), D_MODEL),
        'e_q_lora_norm': gain((N_EVEN, MLA_Q_RANK)),
        'e_kv_lora_norm': gain((N_EVEN, MLA_KV_RANK)),
        'e_w_uq': w((N_EVEN, MLA_Q_RANK, MLA_HEADS * MLA_QK), MLA_Q_RANK),
        'e_w_ukv': w((N_EVEN, MLA_KV_RANK, MLA_HEADS * (MLA_NOPE + MLA_V)), MLA_KV_RANK),
        'e_mla_q_norm': gain((N_EVEN, MLA_QK)),
        'e_mla_k_norm': gain((N_EVEN, MLA_QK)),
        'e_dil_q_norm': gain((N_EVEN, HEAD_DIM)),
        'e_dil_k_norm': gain((N_EVEN, HEAD_DIM)),
        'e_w_out': w((N_EVEN, EVEN_MIX, D_MODEL), EVEN_MIX),
        'o_norm': gain((N_ODD, D_MODEL)),
        'o_w_qkv': w((N_ODD, D_MODEL, 3 * ODD_MIX), D_MODEL),
        'o_q_norm': gain((N_ODD, HEAD_DIM)),
        'o_k_norm': gain((N_ODD, HEAD_DIM)),
        'o_rpb': 0.1 * jax.random.normal(next(ks), (N_ODD, NA_HEADS, 2 * NA_ROWS - 1, 2 * NA_COLS - 1), f32),
        'o_w_out': w((N_ODD, ODD_MIX, D_MODEL), ODD_MIX),
        'f_norm': gain((DEPTH, D_MODEL)),
        'f_w_gate': w((DEPTH, D_MODEL, D_FF), D_MODEL),
        'f_w_up': w((DEPTH, D_MODEL, D_FF), D_MODEL),
        'f_conv_w': w((DEPTH, CONV_WIDTH, D_FF), CONV_WIDTH),
        'f_conv_b': 0.02 * jax.random.normal(next(ks), (DEPTH, D_FF), f32),
        'f_w_down': w((DEPTH, D_FF, D_MODEL), D_FF),
    }


def reference(x, e_norm, e_w_in, e_q_lora_norm, e_kv_lora_norm, e_w_uq, e_w_ukv, e_mla_q_norm,
              e_mla_k_norm, e_dil_q_norm, e_dil_k_norm, e_w_out, o_norm, o_w_qkv, o_q_norm,
              o_k_norm, o_rpb, o_w_out, f_norm, f_w_gate, f_w_up, f_conv_w, f_conv_b, f_w_down):
    S = x.shape[1]
    pos = jnp.arange(S, dtype=jnp.int32)
    for layer in range(DEPTH):
        i = layer // 2
        if layer % 2 == 0:
            x = x + _even_mixer(_rmsnorm(x, e_norm[i]), pos, e_w_in[i], e_q_lora_norm[i],
                                e_kv_lora_norm[i], e_w_uq[i], e_w_ukv[i], e_mla_q_norm[i],
                                e_mla_k_norm[i], e_dil_q_norm[i], e_dil_k_norm[i], e_w_out[i])
        else:
            x = x + _odd_mixer(_rmsnorm(x, o_norm[i]), o_w_qkv[i], o_q_norm[i], o_k_norm[i],
                               o_rpb[i], o_w_out[i])
        x = x + _conv_ffn(_rmsnorm(x, f_norm[layer]), f_w_gate[layer], f_w_up[layer],
                          f_conv_w[layer], f_conv_b[layer], f_w_down[layer])
    return x
```

```python
import functools

import jax
import jax.numpy as jnp
import numpy as np
from jax import lax
from jax.experimental import pallas as pl
from jax.experimental.pallas import tpu as pltpu

F32 = jnp.float32
BF16 = jnp.bfloat16

HEAD_DIM = 128
ROPE_THETA = 500000.0
NORM_EPS = 1e-6
NEG_INF = -1e30

MLA_HEADS = 16
MLA_Q_RANK = 896
MLA_KV_RANK = 512
MLA_NOPE = 128
MLA_ROPE = 64
MLA_QK = MLA_NOPE + MLA_ROPE
MLA_V = 128
MLA_HEAD_PAD = 256
LAT_PAD = 1536

DIL_CONFIGS = ((128, 1), (512, 4), (2048, 16))
DIL_HPG = 8
DIL_HEADS = 24
DIL_ROT = HEAD_DIM // 4
DIL_RADIUS = 64

GRID_W = 64
NA_HEADS = 32
NA_ROWS = 8
NA_COLS = 16
NA_QROWS = 4
NA_KROWS = 12

D_FF = 11008
D_FF_PAD = 11264

VMEM_LIMIT_BYTES = 56 * 1024 * 1024


def _cparams(sem):
    return pltpu.CompilerParams(dimension_semantics=sem, vmem_limit_bytes=VMEM_LIMIT_BYTES)


def _dot(a, b):
    return jnp.dot(a, b, preferred_element_type=F32)


def _dot_nt(a, b):
    return lax.dot_general(a, b, (((1,), (1,)), ((), ())), preferred_element_type=F32)


def _rmsnorm_kernel(x_ref, g_ref, o_ref):
    x = x_ref[...]
    y = x * lax.rsqrt(jnp.mean(x * x, axis=-1, keepdims=True) + NORM_EPS)
    o_ref[...] = (y * g_ref[...]).astype(o_ref.dtype)


def _rmsnorm(x2d, g, bm=512):
    m, d = x2d.shape
    return pl.pallas_call(
        _rmsnorm_kernel,
        grid=(m // bm,),
        in_specs=[pl.BlockSpec((bm, d), lambda i: (i, 0)),
                  pl.BlockSpec((1, d), lambda i: (0, 0))],
        out_specs=pl.BlockSpec((bm, d), lambda i: (i, 0)),
        out_shape=jax.ShapeDtypeStruct((m, d), BF16),
        compiler_params=_cparams(("parallel",)),
        name="rmsnorm",
    )(x2d, g.reshape(1, d))


def _mm_kernel(a_ref, w_ref, o_ref):
    o_ref[...] = _dot(a_ref[...], w_ref[...]).astype(o_ref.dtype)


def _mm(a, w, out_dtype, bm=1024, bn=1024, name="mm"):
    m, k = a.shape
    n = w.shape[1]
    return pl.pallas_call(
        _mm_kernel,
        grid=(m // bm, n // bn),
        in_specs=[pl.BlockSpec((bm, k), lambda i, j: (i, 0)),
                  pl.BlockSpec((k, bn), lambda i, j: (0, j))],
        out_specs=pl.BlockSpec((bm, bn), lambda i, j: (i, j)),
        out_shape=jax.ShapeDtypeStruct((m, n), out_dtype),
        compiler_params=_cparams(("parallel", "parallel")),
        name=name,
    )(a, w)


def _mm_res_kernel(a_ref, w_ref, r_ref, o_ref):
    o_ref[...] = r_ref[...] + _dot(a_ref[...], w_ref[...])


def _mm_res(a, w, res, bm=1024, bn=512, name="mm_res"):
    m, k = a.shape
    n = w.shape[1]
    return pl.pallas_call(
        _mm_res_kernel,
        grid=(m // bm, n // bn),
        in_specs=[pl.BlockSpec((bm, k), lambda i, j: (i, 0)),
                  pl.BlockSpec((k, bn), lambda i, j: (0, j)),
                  pl.BlockSpec((bm, bn), lambda i, j: (i, j))],
        out_specs=pl.BlockSpec((bm, bn), lambda i, j: (i, j)),
        out_shape=jax.ShapeDtypeStruct((m, n), F32),
        compiler_params=_cparams(("parallel", "parallel")),
        name=name,
    )(a, w, res)


def _mm_res_acc_kernel(a_ref, w_ref, r_ref, o_ref, acc_ref):
    k = pl.program_id(2)

    @pl.when(k == 0)
    def _():
        acc_ref[...] = jnp.zeros(acc_ref.shape, F32)

    acc_ref[...] += _dot(a_ref[...], w_ref[...])

    @pl.when(k == pl.num_programs(2) - 1)
    def _():
        o_ref[...] = r_ref[...] + acc_ref[...]


def _mm_res_acc(a, w, res, bk, bm=1024, bn=1024, name="mm_res_acc"):
    m, k = a.shape
    n = w.shape[1]
    return pl.pallas_call(
        _mm_res_acc_kernel,
        grid=(m // bm, n // bn, k // bk),
        in_specs=[pl.BlockSpec((bm, bk), lambda i, j, kk: (i, kk)),
                  pl.BlockSpec((bk, bn), lambda i, j, kk: (kk, j)),
                  pl.BlockSpec((bm, bn), lambda i, j, kk: (i, j))],
        out_specs=pl.BlockSpec((bm, bn), lambda i, j, kk: (i, j)),
        out_shape=jax.ShapeDtypeStruct((m, n), F32),
        scratch_shapes=[pltpu.VMEM((bm, bn), F32)],
        compiler_params=_cparams(("parallel", "parallel", "arbitrary")),
        name=name,
    )(a, w, res)


def _rope_tables(seq, rot, width):
    half = rot // 2
    inv_freq = 1.0 / (ROPE_THETA ** (jnp.arange(half, dtype=F32) * (2.0 / rot)))
    ang = jnp.arange(seq, dtype=jnp.int32).astype(F32)[:, None] * inv_freq[None, :]
    cos, sin = jnp.cos(ang), jnp.sin(ang)
    zh = jnp.zeros((seq, half), F32)
    rest = width - rot
    c = jnp.concatenate([cos, cos, jnp.ones((seq, rest), F32)], axis=1)
    sa = jnp.concatenate([zh, sin, jnp.zeros((seq, rest), F32)], axis=1)
    sb = jnp.concatenate([-sin, zh, jnp.zeros((seq, rest), F32)], axis=1)
    return c, sa, sb


def _apply_rope(y, c, sa, sb, half):
    width = y.shape[-1]
    return y * c + pltpu.roll(y, half, 1) * sa + pltpu.roll(y, width - half, 1) * sb


def _lat_kernel(a_ref, w_ref, gq_ref, gkv_ref, cq_ref, ckv_ref, kpe_ref):
    z = _dot(a_ref[...], w_ref[...])
    c1 = MLA_Q_RANK
    c2 = c1 + MLA_KV_RANK
    cq = z[:, :c1]
    ckv = z[:, c1:c2]
    cq = cq * lax.rsqrt(jnp.mean(cq * cq, axis=-1, keepdims=True) + NORM_EPS)
    ckv = ckv * lax.rsqrt(jnp.mean(ckv * ckv, axis=-1, keepdims=True) + NORM_EPS)
    cq_ref[...] = (cq * gq_ref[...]).astype(cq_ref.dtype)
    ckv_ref[...] = (ckv * gkv_ref[...]).astype(ckv_ref.dtype)
    kpe_ref[...] = z[:, c2:]


def _lat_proj(h, w_lat, gq, gkv, bm=512):
    m, k = h.shape
    return pl.pallas_call(
        _lat_kernel,
        grid=(m // bm,),
        in_specs=[pl.BlockSpec((bm, k), lambda i: (i, 0)),
                  pl.BlockSpec((k, LAT_PAD), lambda i: (0, 0)),
                  pl.BlockSpec((1, MLA_Q_RANK), lambda i: (0, 0)),
                  pl.BlockSpec((1, MLA_KV_RANK), lambda i: (0, 0))],
        out_specs=[pl.BlockSpec((bm, MLA_Q_RANK), lambda i: (i, 0)),
                   pl.BlockSpec((bm, MLA_KV_RANK), lambda i: (i, 0)),
                   pl.BlockSpec((bm, 128), lambda i: (i, 0))],
        out_shape=[jax.ShapeDtypeStruct((m, MLA_Q_RANK), BF16),
                   jax.ShapeDtypeStruct((m, MLA_KV_RANK), BF16),
                   jax.ShapeDtypeStruct((m, 128), F32)],
        compiler_params=_cparams(("parallel",)),
        name="lat_proj",
    )(h, w_lat, gq.reshape(1, -1), gkv.reshape(1, -1))


def _mla_q_kernel(a_ref, w_ref, g_ref, c_ref, sa_ref, sb_ref, o_ref, *, heads):
    z = _dot(a_ref[...], w_ref[...])
    g = g_ref[...]
    c, sa, sb = c_ref[...], sa_ref[...], sb_ref[...]
    for h in range(heads):
        zh = z[:, h * MLA_HEAD_PAD:(h + 1) * MLA_HEAD_PAD]
        ms = jnp.sum(zh * zh, axis=-1, keepdims=True) * (1.0 / MLA_QK)
        y = zh * lax.rsqrt(ms + NORM_EPS) * g
        o_ref[:, h * MLA_HEAD_PAD:h * MLA_HEAD_PAD + 128] = y[:, :128].astype(o_ref.dtype)
        yr = _apply_rope(y[:, 128:], c, sa, sb, MLA_ROPE // 2)
        o_ref[:, h * MLA_HEAD_PAD + 128:(h + 1) * MLA_HEAD_PAD] = yr.astype(o_ref.dtype)


def _mla_q_proj(cq, w_uq, g_pad, tabs, seq, bm=1024, heads=4):
    m, k = cq.shape
    n = w_uq.shape[1]
    bn = heads * MLA_HEAD_PAD
    nsb = seq // bm
    tab_spec = pl.BlockSpec((bm, 128), lambda i, j: (i % nsb, 0))
    return pl.pallas_call(
        functools.partial(_mla_q_kernel, heads=heads),
        grid=(m // bm, n // bn),
        in_specs=[pl.BlockSpec((bm, k), lambda i, j: (i, 0)),
                  pl.BlockSpec((k, bn), lambda i, j: (0, j)),
                  pl.BlockSpec((1, MLA_HEAD_PAD), lambda i, j: (0, 0)),
                  tab_spec, tab_spec, tab_spec],
        out_specs=pl.BlockSpec((bm, bn), lambda i, j: (i, j)),
        out_shape=jax.ShapeDtypeStruct((m, n), BF16),
        compiler_params=_cparams(("parallel", "parallel")),
        name="mla_q_proj",
    )(cq, w_uq, g_pad, *tabs)


def _mla_kv_kernel(a_ref, w_ref, kpe_ref, g_ref, c_ref, sa_ref, sb_ref, k_ref, v_ref, *, heads):
    z = _dot(a_ref[...], w_ref[...])
    kpe = kpe_ref[...]
    g = g_ref[...]
    g_nope, g_rope = g[:, :128], g[:, 128:]
    c, sa, sb = c_ref[...], sa_ref[...], sb_ref[...]
    ss_pe = jnp.sum(kpe * kpe, axis=-1, keepdims=True)
    for h in range(heads):
        nope = z[:, h * 256:h * 256 + 128]
        ms = (jnp.sum(nope * nope, axis=-1, keepdims=True) + ss_pe) * (1.0 / MLA_QK)
        r = lax.rsqrt(ms + NORM_EPS)
        k_ref[:, h * MLA_HEAD_PAD:h * MLA_HEAD_PAD + 128] = (nope * r * g_nope).astype(k_ref.dtype)
        kr = _apply_rope(kpe * r * g_rope, c, sa, sb, MLA_ROPE // 2)
        k_ref[:, h * MLA_HEAD_PAD + 128:(h + 1) * MLA_HEAD_PAD] = kr.astype(k_ref.dtype)
        v_ref[:, h * 128:(h + 1) * 128] = z[:, h * 256 + 128:(h + 1) * 256].astype(v_ref.dtype)


def _mla_kv_proj(ckv, w_ukv, kpe, g_pad, tabs, seq, bm=1024, heads=4):
    m, k = ckv.shape
    nsb = seq // bm
    tab_spec = pl.BlockSpec((bm, 128), lambda i, j: (i % nsb, 0))
    return pl.pallas_call(
        functools.partial(_mla_kv_kernel, heads=heads),
        grid=(m // bm, MLA_HEADS // heads),
        in_specs=[pl.BlockSpec((bm, k), lambda i, j: (i, 0)),
                  pl.BlockSpec((k, heads * 256), lambda i, j: (0, j)),
                  pl.BlockSpec((bm, 128), lambda i, j: (i, 0)),
                  pl.BlockSpec((1, MLA_HEAD_PAD), lambda i, j: (0, 0)),
                  tab_spec, tab_spec, tab_spec],
        out_specs=[pl.BlockSpec((bm, heads * MLA_HEAD_PAD), lambda i, j: (i, j)),
                   pl.BlockSpec((bm, heads * MLA_V), lambda i, j: (i, j))],
        out_shape=[jax.ShapeDtypeStruct((m, MLA_HEADS * MLA_HEAD_PAD), BF16),
                   jax.ShapeDtypeStruct((m, MLA_HEADS * MLA_V), BF16)],
        compiler_params=_cparams(("parallel", "parallel")),
        name="mla_kv_proj",
    )(ckv, w_ukv, kpe, g_pad, *tabs)


def _heads_kernel(a_ref, w_ref, g_ref, *rest, heads, n_norm, rope_half):
    if rope_half:
        c_ref, sa_ref, sb_ref, o_ref = rest
    else:
        (o_ref,) = rest
    z = _dot(a_ref[...], w_ref[...])
    j = pl.program_id(1)

    @pl.when(j < n_norm)
    def _():
        g = g_ref[0]
        for h in range(heads):
            zh = z[:, h * 128:(h + 1) * 128]
            y = zh * lax.rsqrt(jnp.mean(zh * zh, axis=-1, keepdims=True) + NORM_EPS) * g
            if rope_half:
                y = _apply_rope(y, c_ref[...], sa_ref[...], sb_ref[...], rope_half)
            o_ref[:, h * 128:(h + 1) * 128] = y.astype(o_ref.dtype)

    @pl.when(j >= n_norm)
    def _():
        o_ref[...] = z.astype(o_ref.dtype)


def _heads_proj(a, w, gains, n_q_blocks, tabs, seq, rope_half, bm=1024, bn=1024, name="heads_proj"):
    m, k = a.shape
    n = w.shape[1]
    nsb = seq // bm
    g3 = jnp.concatenate([gains, jnp.ones((1, 128), F32)], axis=0).reshape(3, 1, 128)
    in_specs = [pl.BlockSpec((bm, k), lambda i, j: (i, 0)),
                pl.BlockSpec((k, bn), lambda i, j: (0, j)),
                pl.BlockSpec((1, 1, 128), lambda i, j: (j // n_q_blocks, 0, 0))]
    args = [a, w, g3]
    if rope_half:
        tab_spec = pl.BlockSpec((bm, 128), lambda i, j: (i % nsb, 0))
        in_specs += [tab_spec, tab_spec, tab_spec]
        args += list(tabs)
    return pl.pallas_call(
        functools.partial(_heads_kernel, heads=bn // 128, n_norm=2 * n_q_blocks, rope_half=rope_half),
        grid=(m // bm, n // bn),
        in_specs=in_specs,
        out_specs=pl.BlockSpec((bm, bn), lambda i, j: (i, j)),
        out_shape=jax.ShapeDtypeStruct((m, n), BF16),
        compiler_params=_cparams(("parallel", "parallel")),
        name=name,
    )(*args)


def _flash_kernel(q_ref, k_ref, v_ref, o_ref, m_ref, l_ref, acc_ref, *, scale):
    ki = pl.program_id(3)

    @pl.when(ki == 0)
    def _():
        m_ref[...] = jnp.full(m_ref.shape, -jnp.inf, F32)
        l_ref[...] = jnp.zeros(l_ref.shape, F32)
        acc_ref[...] = jnp.zeros(acc_ref.shape, F32)

    s = _dot_nt(q_ref[...], k_ref[...]) * scale
    m_prev = m_ref[...]
    m_new = jnp.maximum(m_prev, jnp.max(s, axis=-1, keepdims=True))
    alpha = jnp.exp(m_prev - m_new)
    p = jnp.exp(s - m_new)
    l_ref[...] = alpha * l_ref[...] + jnp.sum(p, axis=-1, keepdims=True)
    acc_ref[...] = alpha * acc_ref[...] + _dot(p.astype(v_ref.dtype), v_ref[...])
    m_ref[...] = m_new

    @pl.when(ki == pl.num_programs(3) - 1)
    def _():
        o_ref[...] = (acc_ref[...] / l_ref[...]).astype(o_ref.dtype)


def _mla_attention(q, k, v, batch, seq, tq=1024, tk=1024):
    m = q.shape[0]
    nq, nk = seq // tq, seq // tk
    return pl.pallas_call(
        functools.partial(_flash_kernel, scale=MLA_QK ** -0.5),
        grid=(batch, MLA_HEADS, nq, nk),
        in_specs=[pl.BlockSpec((tq, MLA_HEAD_PAD), lambda b, h, qi, ki: (b * nq + qi, h)),
                  pl.BlockSpec((tk, MLA_HEAD_PAD), lambda b, h, qi, ki: (b * nk + ki, h)),
                  pl.BlockSpec((tk, MLA_V), lambda b, h, qi, ki: (b * nk + ki, h))],
        out_specs=pl.BlockSpec((tq, MLA_V), lambda b, h, qi, ki: (b * nq + qi, h)),
        out_shape=jax.ShapeDtypeStruct((m, MLA_HEADS * MLA_V), BF16),
        scratch_shapes=[pltpu.VMEM((tq, 1), F32), pltpu.VMEM((tq, 1), F32),
                        pltpu.VMEM((tq, MLA_V), F32)],
        compiler_params=_cparams(("parallel", "parallel", "parallel", "arbitrary")),
        name="mla_attention",
    )(q, k, v)


DIL_SQ = 256
DIL_SK = DIL_SQ + 2 * DIL_RADIUS


def _dil_kernel(q_ref, kp_ref, k_ref, kn_ref, vp_ref, v_ref, vn_ref, o_ref, st_ref,
                kwin_ref, vwin_ref, *, tq, length):
    r = DIL_RADIUS
    q0 = pl.program_id(2) * tq
    kwin_ref[0:r] = kp_ref[0]
    kwin_ref[r:r + tq] = k_ref[0]
    kwin_ref[r + tq:] = kn_ref[0]
    vwin_ref[0:r] = vp_ref[0]
    vwin_ref[r:r + tq] = v_ref[0]
    vwin_ref[r + tq:] = vn_ref[0]
    scale = HEAD_DIM ** -0.5
    qi = lax.broadcasted_iota(jnp.int32, (DIL_SQ, DIL_SK), 0)
    kj = lax.broadcasted_iota(jnp.int32, (DIL_SQ, DIL_SK), 1)
    band = (kj >= qi) & (kj <= qi + 2 * r)
    lane = lax.broadcasted_iota(jnp.int32, (1, 128), 1)
    for c in range(tq // DIL_SQ):
        kpos = q0 + c * DIL_SQ - r + kj
        valid = band & (kpos >= 0) & (kpos < length)
        st = jnp.zeros((DIL_SQ, 128), F32)
        for h in range(DIL_HPG):
            hs = slice(h * 128, (h + 1) * 128)
            qh = q_ref[0, c * DIL_SQ:(c + 1) * DIL_SQ, hs]
            kh = kwin_ref[c * DIL_SQ:c * DIL_SQ + DIL_SK, hs]
            vh = vwin_ref[c * DIL_SQ:c * DIL_SQ + DIL_SK, hs]
            s = jnp.where(valid, _dot_nt(qh, kh) * scale, NEG_INF)
            mx = jnp.max(s, axis=-1, keepdims=True)
            p = jnp.exp(s - mx)
            den = jnp.sum(p, axis=-1, keepdims=True)
            o = _dot(p.astype(vh.dtype), vh) / den
            o_ref[0, c * DIL_SQ:(c + 1) * DIL_SQ, hs] = o
            st = jnp.where(lane == h, mx, st)
            st = jnp.where(lane == DIL_HPG + h, den, st)
        st_ref[0, c * DIL_SQ:(c + 1) * DIL_SQ, :] = st


def _dilated_group(zb, batch, seq, group, dil):
    length = seq // dil
    tq = min(512, length)
    r = DIL_RADIUS
    nq = length // tq
    hb = tq // r
    nhb = length // r
    ncol = zb.shape[1] // (DIL_HPG * 128)
    zv = zb.reshape(batch, length, dil * zb.shape[1])
    w = DIL_HPG * 128

    def main(part):
        return pl.BlockSpec((1, tq, w), lambda b, rr, qi: (b, qi, rr * ncol + part * 3 + group))

    def prev(part):
        return pl.BlockSpec((1, r, w), lambda b, rr, qi: (b, jnp.maximum(qi * hb - 1, 0),
                                                         rr * ncol + part * 3 + group))

    def nxt(part):
        return pl.BlockSpec((1, r, w), lambda b, rr, qi: (b, jnp.minimum((qi + 1) * hb, nhb - 1),
                                                         rr * ncol + part * 3 + group))

    o, st = pl.pallas_call(
        functools.partial(_dil_kernel, tq=tq, length=length),
        grid=(batch, dil, nq),
        in_specs=[main(0), prev(1), main(1), nxt(1), prev(2), main(2), nxt(2)],
        out_specs=[pl.BlockSpec((1, tq, w), lambda b, rr, qi: (b, qi, rr)),
                   pl.BlockSpec((1, tq, 128), lambda b, rr, qi: (b, qi, rr))],
        out_shape=[jax.ShapeDtypeStruct((batch, length, dil * w), F32),
                   jax.ShapeDtypeStruct((batch, length, dil * 128), F32)],
        scratch_shapes=[pltpu.VMEM((tq + 2 * r, w), BF16), pltpu.VMEM((tq + 2 * r, w), BF16)],
        compiler_params=_cparams(("parallel", "parallel", "parallel")),
        name=f"dilated_g{group}",
    )(zv, zv, zv, zv, zv, zv, zv)
    return o.reshape(batch * seq, w), st.reshape(batch * seq, 128)


def _dil_merge_kernel(o0_ref, o1_ref, o2_ref, s0_ref, s1_ref, s2_ref, out_ref):
    s0, s1, s2 = s0_ref[...], s1_ref[...], s2_ref[...]
    for h in range(DIL_HPG):
        hs = slice(h * 128, (h + 1) * 128)
        m0, m1, m2 = s0[:, h:h + 1], s1[:, h:h + 1], s2[:, h:h + 1]
        l0, l1, l2 = (s0[:, DIL_HPG + h:DIL_HPG + h + 1], s1[:, DIL_HPG + h:DIL_HPG + h + 1],
                      s2[:, DIL_HPG + h:DIL_HPG + h + 1])
        mx = jnp.maximum(jnp.maximum(m0, m1), m2)
        w0 = jnp.exp(m0 - mx) * l0
        w1 = jnp.exp(m1 - mx) * l1
        w2 = jnp.exp(m2 - mx) * l2
        num = w0 * o0_ref[:, hs] + w1 * o1_ref[:, hs] + w2 * o2_ref[:, hs]
        out_ref[:, hs] = (num / (w0 + w1 + w2)).astype(out_ref.dtype)


def _dil_merge(outs, stats, bm=512):
    m, w = outs[0].shape
    o_spec = pl.BlockSpec((bm, w), lambda i: (i, 0))
    s_spec = pl.BlockSpec((bm, 128), lambda i: (i, 0))
    return pl.pallas_call(
        _dil_merge_kernel,
        grid=(m // bm,),
        in_specs=[o_spec, o_spec, o_spec, s_spec, s_spec, s_spec],
        out_specs=o_spec,
        out_shape=jax.ShapeDtypeStruct((m, w), BF16),
        compiler_params=_cparams(("parallel",)),
        name="dilated_merge",
    )(*outs, *stats)


NA_TQ = NA_QROWS * GRID_W
NA_TK = NA_KROWS * GRID_W
NA_HPS = 8


def _na_bias_tables(rpb, rows):
    heads = rpb.shape[0]
    cols = np.arange(GRID_W)
    cs = np.clip(cols - NA_COLS // 2, 0, GRID_W - NA_COLS)
    kc = np.arange(GRID_W)
    col_ok = (kc[None, :] >= cs[:, None]) & (kc[None, :] < cs[:, None] + NA_COLS)
    col_idx = np.clip(kc[None, :] - cols[:, None] + (NA_COLS - 1), 0, 2 * NA_COLS - 2)
    toe = jnp.where(col_ok[None, None], rpb[:, :, col_idx], NEG_INF)
    tabs = []
    for i0, ks in ((0, 0), (NA_QROWS * 2, NA_QROWS), (rows - NA_QROWS, rows - NA_KROWS)):
        qi = i0 + np.arange(NA_QROWS)
        rs = np.clip(qi - NA_ROWS // 2, 0, rows - NA_ROWS)
        kr = ks + np.arange(NA_KROWS)
        row_ok = (kr[None, :] >= rs[:, None]) & (kr[None, :] < rs[:, None] + NA_ROWS)
        dr = np.clip(kr[None, :] - qi[:, None] + (NA_ROWS - 1), 0, 2 * NA_ROWS - 2)
        t = toe[:, dr]
        t = jnp.where(row_ok[None, :, :, None, None], t, NEG_INF)
        tabs.append(t.transpose(0, 1, 3, 2, 4).reshape(heads, NA_TQ, NA_TK))
    return jnp.stack(tabs)


def _na_kernel(q_ref, k0_ref, k1_ref, k2_ref, v0_ref, v1_ref, v2_ref, b_ref, o_ref):
    scale = HEAD_DIM ** -0.5
    for h in range(NA_HPS):
        hs = slice(h * 128, (h + 1) * 128)
        q = q_ref[0, :, hs]
        k = jnp.concatenate([k0_ref[0, :, hs], k1_ref[0, :, hs], k2_ref[0, :, hs]], axis=0)
        v = jnp.concatenate([v0_ref[0, :, hs], v1_ref[0, :, hs], v2_ref[0, :, hs]], axis=0)
        s = _dot_nt(q, k) * scale + b_ref[0, h]
        mx = jnp.max(s, axis=-1, keepdims=True)
        p = jnp.exp(s - mx)
        den = jnp.sum(p, axis=-1, keepdims=True)
        o = _dot(p.astype(v.dtype), v) / den
        o_ref[0, :, hs] = o.astype(o_ref.dtype)


def _na_attention(qkv, bias, batch, seq):
    rows = seq // GRID_W
    nblk = rows // NA_QROWS
    nkb = NA_KROWS // NA_QROWS
    ng = NA_HEADS // NA_HPS
    w = NA_HPS * 128
    qv = qkv.reshape(batch, seq, qkv.shape[1])

    def kv_spec(part, off):
        return pl.BlockSpec(
            (1, NA_TQ, w),
            lambda b, g, rb: (b, jnp.clip(rb - 1, 0, nblk - nkb) + off, part * ng + g))

    def bias_map(b, g, rb):
        cfg = jnp.where(rb == 0, 0, jnp.where(rb == nblk - 1, 2, 1))
        return (cfg, g, 0, 0)

    o = pl.pallas_call(
        _na_kernel,
        grid=(batch, ng, nblk),
        in_specs=[pl.BlockSpec((1, NA_TQ, w), lambda b, g, rb: (b, rb, g)),
                  kv_spec(1, 0), kv_spec(1, 1), kv_spec(1, 2),
                  kv_spec(2, 0), kv_spec(2, 1), kv_spec(2, 2),
                  pl.BlockSpec((1, NA_HPS, NA_TQ, NA_TK), bias_map)],
        out_specs=pl.BlockSpec((1, NA_TQ, w), lambda b, g, rb: (b, rb, g)),
        out_shape=jax.ShapeDtypeStruct((batch, seq, NA_HEADS * 128), BF16),
        compiler_params=_cparams(("parallel", "parallel", "arbitrary")),
        name="na_attention",
    )(qv, qv, qv, qv, qv, qv, qv, bias)
    return o.reshape(batch * seq, NA_HEADS * 128)


def _ffn_up_kernel(h_ref, w_ref, a_ref, ap_ref, an_ref, cw_ref, cb_ref, o_ref, *, bm, seq):
    i = pl.program_id(0)
    u = _dot(h_ref[...], w_ref[...])
    a = a_ref[...]
    first = (i * bm) % seq == 0
    last = ((i + 1) * bm) % seq == 0
    prev_row = jnp.where(first, 0.0, ap_ref[7:8, :])
    next_row = jnp.where(last, 0.0, an_ref[0:1, :])
    ridx = lax.broadcasted_iota(jnp.int32, (bm, 1), 0)
    a_prev = jnp.where(ridx == 0, prev_row, pltpu.roll(a, 1, 0))
    a_next = jnp.where(ridx == bm - 1, next_row, pltpu.roll(a, bm - 1, 0))
    cw = cw_ref[...]
    ac = a_prev * cw[0:1] + a * cw[1:2] + a_next * cw[2:3] + cb_ref[...]
    o_ref[...] = (ac * jax.nn.sigmoid(ac) * u).astype(o_ref.dtype)


def _ffn_up(h, w_up, a, conv_w, conv_b, seq, bm=1024, bn=512):
    m, k = h.shape
    n = w_up.shape[1]
    rb = bm // 8
    nrb = m // 8
    return pl.pallas_call(
        functools.partial(_ffn_up_kernel, bm=bm, seq=seq),
        grid=(m // bm, n // bn),
        in_specs=[pl.BlockSpec((bm, k), lambda i, j: (i, 0)),
                  pl.BlockSpec((k, bn), lambda i, j: (0, j)),
                  pl.BlockSpec((bm, bn), lambda i, j: (i, j)),
                  pl.BlockSpec((8, bn), lambda i, j: (jnp.maximum(i * rb - 1, 0), j)),
                  pl.BlockSpec((8, bn), lambda i, j: (jnp.minimum((i + 1) * rb, nrb - 1), j)),
                  pl.BlockSpec((3, bn), lambda i, j: (0, j)),
                  pl.BlockSpec((1, bn), lambda i, j: (0, j))],
        out_specs=pl.BlockSpec((bm, bn), lambda i, j: (i, j)),
        out_shape=jax.ShapeDtypeStruct((m, n), BF16),
        compiler_params=_cparams(("parallel", "parallel")),
        name="ffn_up",
    )(h, w_up, a, a, a, conv_w, conv_b)


def _conv_ffn(x2d, seq, norm_g, w_gate, w_up, conv_w, conv_b, w_down):
    pad = D_FF_PAD - D_FF
    wg = jnp.pad(w_gate, ((0, 0), (0, pad))).astype(BF16)
    wu = jnp.pad(w_up, ((0, 0), (0, pad))).astype(BF16)
    wd = jnp.pad(w_down, ((0, pad), (0, 0))).astype(BF16)
    cw = jnp.pad(conv_w, ((0, 0), (0, pad)))
    cb = jnp.pad(conv_b, ((0, pad),)).reshape(1, D_FF_PAD)
    h = _rmsnorm(x2d, norm_g)
    a = _mm(h, wg, F32, name="ffn_gate")
    g = _ffn_up(h, wu, a, cw, cb, seq)
    return _mm_res_acc(g, wd, x2d, bk=D_FF_PAD // 4, name="ffn_down")


def _even_layer(x2d, batch, seq, e_norm, w_in, q_lora_g, kv_lora_g, w_uq, w_ukv, mla_qn, mla_kn,
                dil_qn, dil_kn, w_out):
    c3 = MLA_Q_RANK + MLA_KV_RANK + MLA_ROPE
    w_lat = jnp.pad(w_in[:, :c3], ((0, 0), (0, LAT_PAD - c3))).astype(BF16)
    w_b = w_in[:, c3:].astype(BF16)
    hpad = MLA_HEAD_PAD - MLA_QK
    w_uq_p = jnp.pad(w_uq.reshape(MLA_Q_RANK, MLA_HEADS, MLA_QK), ((0, 0), (0, 0), (0, hpad)))
    w_uq_p = w_uq_p.reshape(MLA_Q_RANK, MLA_HEADS * MLA_HEAD_PAD).astype(BF16)
    gq_pad = jnp.pad(mla_qn, (0, hpad)).reshape(1, MLA_HEAD_PAD)
    gk_pad = jnp.pad(mla_kn, (0, hpad)).reshape(1, MLA_HEAD_PAD)
    mla_tabs = _rope_tables(seq, MLA_ROPE, 128)
    dil_tabs = _rope_tables(seq, DIL_ROT, 128)

    h = _rmsnorm(x2d, e_norm)
    cq, ckv, kpe = _lat_proj(h, w_lat, q_lora_g, kv_lora_g)
    q = _mla_q_proj(cq, w_uq_p, gq_pad, mla_tabs, seq)
    k, v = _mla_kv_proj(ckv, w_ukv.astype(BF16), kpe, gk_pad, mla_tabs, seq)
    o_a = _mla_attention(q, k, v, batch, seq)

    zb = _heads_proj(h, w_b, jnp.stack([dil_qn, dil_kn]), DIL_HEADS * 128 // 1024, dil_tabs, seq,
                     DIL_ROT // 2, name="dil_qkv_proj")
    outs, stats = [], []
    for g, (window, dil) in enumerate(DIL_CONFIGS):
        assert window // (2 * dil) == DIL_RADIUS
        o, st = _dilated_group(zb, batch, seq, g, dil)
        outs.append(o)
        stats.append(st)
    o_b = _dil_merge(outs, stats)
    mix = jnp.concatenate([o_a, o_b], axis=1)
    return _mm_res(mix, w_out.astype(BF16), x2d, name="even_out_proj")


def _odd_layer(x2d, batch, seq, o_norm, w_qkv, qn, kn, rpb, w_out):
    h = _rmsnorm(x2d, o_norm)
    qkv = _heads_proj(h, w_qkv.astype(BF16), jnp.stack([qn, kn]), NA_HEADS * 128 // 1024, None, seq, 0,
                      name="na_qkv_proj")
    bias = _na_bias_tables(rpb, seq // GRID_W)
    o = _na_attention(qkv, bias, batch, seq)
    return _mm_res(o, w_out.astype(BF16), x2d, name="odd_out_proj")


def kernel(x, e_norm, e_w_in, e_q_lora_norm, e_kv_lora_norm, e_w_uq, e_w_ukv, e_mla_q_norm,
           e_mla_k_norm, e_dil_q_norm, e_dil_k_norm, e_w_out, o_norm, o_w_qkv, o_q_norm, o_k_norm,
           o_rpb, o_w_out, f_norm, f_w_gate, f_w_up, f_conv_w, f_conv_b, f_w_down):
    batch, seq, d = x.shape
    x2d = x.reshape(batch * seq, d)
    depth = f_norm.shape[0]
    for layer in range(depth):
        i = layer // 2
        if layer % 2 == 0:
            x2d = _even_layer(x2d, batch, seq, e_norm[i], e_w_in[i], e_q_lora_norm[i],
                              e_kv_lora_norm[i], e_w_uq[i], e_w_ukv[i], e_mla_q_norm[i],
                              e_mla_k_norm[i], e_dil_q_norm[i], e_dil_k_norm[i], e_w_out[i])
        else:
            x2d = _odd_layer(x2d, batch, seq, o_norm[i], o_w_qkv[i], o_q_norm[i], o_k_norm[i],
                             o_rpb[i], o_w_out[i])
        x2d = _conv_ffn(x2d, seq, f_norm[layer], f_w_gate[layer], f_w_up[layer], f_conv_w[layer],
                        f_conv_b[layer], f_w_down[layer])
    return x2d.reshape(batch, seq, d)
```

```python
import functools

import jax
import jax.numpy as jnp
import numpy as np
from jax import lax
from jax.experimental import pallas as pl
from jax.experimental.pallas import tpu as pltpu

F32 = jnp.float32
BF16 = jnp.bfloat16

HEAD_DIM = 128
ROPE_THETA = 500000.0
NORM_EPS = 1e-6
NEG_INF = -1e30

MLA_HEADS = 16
MLA_Q_RANK = 896
MLA_KV_RANK = 512
MLA_NOPE = 128
MLA_ROPE = 64
MLA_QK = MLA_NOPE + MLA_ROPE
MLA_V = 128
MLA_HEAD_PAD = 256
LAT_PAD = 1536

DIL_CONFIGS = ((128, 1), (512, 4), (2048, 16))
DIL_HPG = 8
DIL_HEADS = 24
DIL_ROT = HEAD_DIM // 4
DIL_RADIUS = 64

GRID_W = 64
NA_HEADS = 32
NA_ROWS = 8
NA_COLS = 16
NA_QROWS = 4
NA_KROWS = 12

D_FF = 11008
D_FF_PAD = 11264

VMEM_LIMIT_BYTES = 56 * 1024 * 1024


def _cparams(sem):
    return pltpu.CompilerParams(dimension_semantics=sem, vmem_limit_bytes=VMEM_LIMIT_BYTES)


def _dot(a, b):
    return jnp.dot(a, b, preferred_element_type=F32)


def _dot_nt(a, b):
    return lax.dot_general(a, b, (((1,), (1,)), ((), ())), preferred_element_type=F32)


def _rmsnorm_kernel(x_ref, g_ref, o_ref):
    x = x_ref[...]
    y = x * lax.rsqrt(jnp.mean(x * x, axis=-1, keepdims=True) + NORM_EPS)
    o_ref[...] = (y * g_ref[...]).astype(o_ref.dtype)


def _rmsnorm(x2d, g, bm=512):
    m, d = x2d.shape
    return pl.pallas_call(
        _rmsnorm_kernel,
        grid=(m // bm,),
        in_specs=[pl.BlockSpec((bm, d), lambda i: (i, 0)),
                  pl.BlockSpec((1, d), lambda i: (0, 0))],
        out_specs=pl.BlockSpec((bm, d), lambda i: (i, 0)),
        out_shape=jax.ShapeDtypeStruct((m, d), BF16),
        compiler_params=_cparams(("parallel",)),
        name="rmsnorm",
    )(x2d, g.reshape(1, d))


def _cast_kernel(w_ref, o_ref, *, n_in_blocks, cols):
    i = pl.program_id(0)

    @pl.when(i < n_in_blocks)
    def _():
        o_ref[:, :cols] = w_ref[...].astype(o_ref.dtype)
        if o_ref.shape[1] > cols:
            o_ref[:, cols:] = jnp.zeros((o_ref.shape[0], o_ref.shape[1] - cols), o_ref.dtype)

    @pl.when(i >= n_in_blocks)
    def _():
        o_ref[...] = jnp.zeros(o_ref.shape, o_ref.dtype)


def _cast_pad(w, rows_out, cols_out, br=256):
    rows, cols = w.shape
    nin = rows // br
    return pl.pallas_call(
        functools.partial(_cast_kernel, n_in_blocks=nin, cols=cols),
        grid=(rows_out // br,),
        in_specs=[pl.BlockSpec((br, cols), lambda i: (jnp.minimum(i, nin - 1), 0))],
        out_specs=pl.BlockSpec((br, cols_out), lambda i: (i, 0)),
        out_shape=jax.ShapeDtypeStruct((rows_out, cols_out), BF16),
        compiler_params=_cparams(("parallel",)),
        name="cast_pad",
    )(w)


def _mm_kernel(a_ref, w_ref, o_ref):
    o_ref[...] = _dot(a_ref[...], w_ref[...]).astype(o_ref.dtype)


def _mm(a, w, out_dtype, bm=1024, bn=1024, name="mm"):
    m, k = a.shape
    n = w.shape[1]
    return pl.pallas_call(
        _mm_kernel,
        grid=(m // bm, n // bn),
        in_specs=[pl.BlockSpec((bm, k), lambda i, j: (i, 0)),
                  pl.BlockSpec((k, bn), lambda i, j: (0, j))],
        out_specs=pl.BlockSpec((bm, bn), lambda i, j: (i, j)),
        out_shape=jax.ShapeDtypeStruct((m, n), out_dtype),
        compiler_params=_cparams(("parallel", "parallel")),
        name=name,
    )(a, w)


def _mm_res_kernel(a_ref, w_ref, r_ref, o_ref):
    o_ref[...] = r_ref[...] + _dot(a_ref[...], w_ref[...])


def _mm_res(a, w, res, bm=1024, bn=512, name="mm_res"):
    m, k = a.shape
    n = w.shape[1]
    return pl.pallas_call(
        _mm_res_kernel,
        grid=(m // bm, n // bn),
        in_specs=[pl.BlockSpec((bm, k), lambda i, j: (i, 0)),
                  pl.BlockSpec((k, bn), lambda i, j: (0, j)),
                  pl.BlockSpec((bm, bn), lambda i, j: (i, j))],
        out_specs=pl.BlockSpec((bm, bn), lambda i, j: (i, j)),
        out_shape=jax.ShapeDtypeStruct((m, n), F32),
        compiler_params=_cparams(("parallel", "parallel")),
        name=name,
    )(a, w, res)


def _mm_res2_kernel(a1_ref, a2_ref, w1_ref, w2_ref, r_ref, o_ref):
    o_ref[...] = r_ref[...] + _dot(a1_ref[...], w1_ref[...]) + _dot(a2_ref[...], w2_ref[...])


def _mm_res2(a1, a2, w, res, bm=1024, bn=512, name="mm_res2"):
    m, k1 = a1.shape
    k2 = a2.shape[1]
    n = w.shape[1]
    assert k1 % k2 == 0 and w.shape[0] == k1 + k2
    return pl.pallas_call(
        _mm_res2_kernel,
        grid=(m // bm, n // bn),
        in_specs=[pl.BlockSpec((bm, k1), lambda i, j: (i, 0)),
                  pl.BlockSpec((bm, k2), lambda i, j: (i, 0)),
                  pl.BlockSpec((k1, bn), lambda i, j: (0, j)),
                  pl.BlockSpec((k2, bn), lambda i, j: (k1 // k2, j)),
                  pl.BlockSpec((bm, bn), lambda i, j: (i, j))],
        out_specs=pl.BlockSpec((bm, bn), lambda i, j: (i, j)),
        out_shape=jax.ShapeDtypeStruct((m, n), F32),
        compiler_params=_cparams(("parallel", "parallel")),
        name=name,
    )(a1, a2, w, w, res)


def _mm_res_acc_kernel(a_ref, w_ref, r_ref, o_ref, acc_ref):
    k = pl.program_id(2)

    @pl.when(k == 0)
    def _():
        acc_ref[...] = jnp.zeros(acc_ref.shape, F32)

    acc_ref[...] += _dot(a_ref[...], w_ref[...])

    @pl.when(k == pl.num_programs(2) - 1)
    def _():
        o_ref[...] = r_ref[...] + acc_ref[...]


def _mm_res_acc(a, w, res, bk, bm=1024, bn=1024, name="mm_res_acc"):
    m, k = a.shape
    n = w.shape[1]
    return pl.pallas_call(
        _mm_res_acc_kernel,
        grid=(m // bm, n // bn, k // bk),
        in_specs=[pl.BlockSpec((bm, bk), lambda i, j, kk: (i, kk)),
                  pl.BlockSpec((bk, bn), lambda i, j, kk: (kk, j)),
                  pl.BlockSpec((bm, bn), lambda i, j, kk: (i, j))],
        out_specs=pl.BlockSpec((bm, bn), lambda i, j, kk: (i, j)),
        out_shape=jax.ShapeDtypeStruct((m, n), F32),
        scratch_shapes=[pltpu.VMEM((bm, bn), F32)],
        compiler_params=_cparams(("parallel", "parallel", "arbitrary")),
        name=name,
    )(a, w, res)


def _rope_tables(seq, rot, width):
    half = rot // 2
    inv_freq = 1.0 / (ROPE_THETA ** (jnp.arange(half, dtype=F32) * (2.0 / rot)))
    ang = jnp.arange(seq, dtype=jnp.int32).astype(F32)[:, None] * inv_freq[None, :]
    cos, sin = jnp.cos(ang), jnp.sin(ang)
    zh = jnp.zeros((seq, half), F32)
    rest = width - rot
    c = jnp.concatenate([cos, cos, jnp.ones((seq, rest), F32)], axis=1)
    sa = jnp.concatenate([zh, sin, jnp.zeros((seq, rest), F32)], axis=1)
    sb = jnp.concatenate([-sin, zh, jnp.zeros((seq, rest), F32)], axis=1)
    return c, sa, sb


def _apply_rope(y, c, sa, sb, half):
    width = y.shape[-1]
    return y * c + pltpu.roll(y, half, 1) * sa + pltpu.roll(y, width - half, 1) * sb


def _lat_kernel(a_ref, w_ref, gq_ref, gkv_ref, cq_ref, ckv_ref, kpe_ref):
    z = _dot(a_ref[...], w_ref[...])
    c1 = MLA_Q_RANK
    c2 = c1 + MLA_KV_RANK
    cq = z[:, :c1]
    ckv = z[:, c1:c2]
    cq = cq * lax.rsqrt(jnp.mean(cq * cq, axis=-1, keepdims=True) + NORM_EPS)
    ckv = ckv * lax.rsqrt(jnp.mean(ckv * ckv, axis=-1, keepdims=True) + NORM_EPS)
    cq_ref[...] = (cq * gq_ref[...]).astype(cq_ref.dtype)
    ckv_ref[...] = (ckv * gkv_ref[...]).astype(ckv_ref.dtype)
    kpe_ref[...] = z[:, c2:]


def _lat_proj(h, w_lat, gq, gkv, bm=512):
    m, k = h.shape
    return pl.pallas_call(
        _lat_kernel,
        grid=(m // bm,),
        in_specs=[pl.BlockSpec((bm, k), lambda i: (i, 0)),
                  pl.BlockSpec((k, LAT_PAD), lambda i: (0, 0)),
                  pl.BlockSpec((1, MLA_Q_RANK), lambda i: (0, 0)),
                  pl.BlockSpec((1, MLA_KV_RANK), lambda i: (0, 0))],
        out_specs=[pl.BlockSpec((bm, MLA_Q_RANK), lambda i: (i, 0)),
                   pl.BlockSpec((bm, MLA_KV_RANK), lambda i: (i, 0)),
                   pl.BlockSpec((bm, 128), lambda i: (i, 0))],
        out_shape=[jax.ShapeDtypeStruct((m, MLA_Q_RANK), BF16),
                   jax.ShapeDtypeStruct((m, MLA_KV_RANK), BF16),
                   jax.ShapeDtypeStruct((m, 128), F32)],
        compiler_params=_cparams(("parallel",)),
        name="lat_proj",
    )(h, w_lat, gq.reshape(1, -1), gkv.reshape(1, -1))


def _mla_q_kernel(a_ref, w_ref, g_ref, c_ref, sa_ref, sb_ref, o_ref, *, heads):
    z = _dot(a_ref[...], w_ref[...])
    g = g_ref[...]
    c, sa, sb = c_ref[...], sa_ref[...], sb_ref[...]
    for h in range(heads):
        zh = z[:, h * MLA_HEAD_PAD:(h + 1) * MLA_HEAD_PAD]
        ms = jnp.sum(zh * zh, axis=-1, keepdims=True) * (1.0 / MLA_QK)
        y = zh * lax.rsqrt(ms + NORM_EPS) * g
        o_ref[:, h * MLA_HEAD_PAD:h * MLA_HEAD_PAD + 128] = y[:, :128].astype(o_ref.dtype)
        yr = _apply_rope(y[:, 128:], c, sa, sb, MLA_ROPE // 2)
        o_ref[:, h * MLA_HEAD_PAD + 128:(h + 1) * MLA_HEAD_PAD] = yr.astype(o_ref.dtype)


def _mla_q_proj(cq, w_uq, g_pad, tabs, seq, bm=1024, heads=4):
    m, k = cq.shape
    n = w_uq.shape[1]
    bn = heads * MLA_HEAD_PAD
    nsb = seq // bm
    tab_spec = pl.BlockSpec((bm, 128), lambda i, j: (i % nsb, 0))
    return pl.pallas_call(
        functools.partial(_mla_q_kernel, heads=heads),
        grid=(m // bm, n // bn),
        in_specs=[pl.BlockSpec((bm, k), lambda i, j: (i, 0)),
                  pl.BlockSpec((k, bn), lambda i, j: (0, j)),
                  pl.BlockSpec((1, MLA_HEAD_PAD), lambda i, j: (0, 0)),
                  tab_spec, tab_spec, tab_spec],
        out_specs=pl.BlockSpec((bm, bn), lambda i, j: (i, j)),
        out_shape=jax.ShapeDtypeStruct((m, n), BF16),
        compiler_params=_cparams(("parallel", "parallel")),
        name="mla_q_proj",
    )(cq, w_uq, g_pad, *tabs)


def _mla_kv_kernel(a_ref, w_ref, kpe_ref, g_ref, c_ref, sa_ref, sb_ref, k_ref, v_ref, *, heads):
    z = _dot(a_ref[...], w_ref[...])
    kpe = kpe_ref[...]
    g = g_ref[...]
    g_nope, g_rope = g[:, :128], g[:, 128:]
    c, sa, sb = c_ref[...], sa_ref[...], sb_ref[...]
    ss_pe = jnp.sum(kpe * kpe, axis=-1, keepdims=True)
    for h in range(heads):
        nope = z[:, h * 256:h * 256 + 128]
        ms = (jnp.sum(nope * nope, axis=-1, keepdims=True) + ss_pe) * (1.0 / MLA_QK)
        r = lax.rsqrt(ms + NORM_EPS)
        k_ref[:, h * MLA_HEAD_PAD:h * MLA_HEAD_PAD + 128] = (nope * r * g_nope).astype(k_ref.dtype)
        kr = _apply_rope(kpe * r * g_rope, c, sa, sb, MLA_ROPE // 2)
        k_ref[:, h * MLA_HEAD_PAD + 128:(h + 1) * MLA_HEAD_PAD] = kr.astype(k_ref.dtype)
        v_ref[:, h * 128:(h + 1) * 128] = z[:, h * 256 + 128:(h + 1) * 256].astype(v_ref.dtype)


def _mla_kv_proj(ckv, w_ukv, kpe, g_pad, tabs, seq, bm=1024, heads=4):
    m, k = ckv.shape
    nsb = seq // bm
    tab_spec = pl.BlockSpec((bm, 128), lambda i, j: (i % nsb, 0))
    return pl.pallas_call(
        functools.partial(_mla_kv_kernel, heads=heads),
        grid=(m // bm, MLA_HEADS // heads),
        in_specs=[pl.BlockSpec((bm, k), lambda i, j: (i, 0)),
                  pl.BlockSpec((k, heads * 256), lambda i, j: (0, j)),
                  pl.BlockSpec((bm, 128), lambda i, j: (i, 0)),
                  pl.BlockSpec((1, MLA_HEAD_PAD), lambda i, j: (0, 0)),
                  tab_spec, tab_spec, tab_spec],
        out_specs=[pl.BlockSpec((bm, heads * MLA_HEAD_PAD), lambda i, j: (i, j)),
                   pl.BlockSpec((bm, heads * MLA_V), lambda i, j: (i, j))],
        out_shape=[jax.ShapeDtypeStruct((m, MLA_HEADS * MLA_HEAD_PAD), BF16),
                   jax.ShapeDtypeStruct((m, MLA_HEADS * MLA_V), BF16)],
        compiler_params=_cparams(("parallel", "parallel")),
        name="mla_kv_proj",
    )(ckv, w_ukv, kpe, g_pad, *tabs)


def _heads_kernel(a_ref, w_ref, g_ref, *rest, heads, n_norm, rope_half):
    if rope_half:
        c_ref, sa_ref, sb_ref, o_ref = rest
    else:
        (o_ref,) = rest
    z = _dot(a_ref[...], w_ref[...])
    j = pl.program_id(1)

    @pl.when(j < n_norm)
    def _():
        g = g_ref[0]
        for h in range(heads):
            zh = z[:, h * 128:(h + 1) * 128]
            y = zh * lax.rsqrt(jnp.mean(zh * zh, axis=-1, keepdims=True) + NORM_EPS) * g
            if rope_half:
                y = _apply_rope(y, c_ref[...], sa_ref[...], sb_ref[...], rope_half)
            o_ref[:, h * 128:(h + 1) * 128] = y.astype(o_ref.dtype)

    @pl.when(j >= n_norm)
    def _():
        o_ref[...] = z.astype(o_ref.dtype)


def _heads_proj(a, w, gains, n_q_blocks, tabs, seq, rope_half, bm=1024, bn=1024, name="heads_proj"):
    m, k = a.shape
    n = w.shape[1]
    nsb = seq // bm
    g3 = jnp.concatenate([gains, jnp.ones((1, 128), F32)], axis=0).reshape(3, 1, 128)
    in_specs = [pl.BlockSpec((bm, k), lambda i, j: (i, 0)),
                pl.BlockSpec((k, bn), lambda i, j: (0, j)),
                pl.BlockSpec((1, 1, 128), lambda i, j: (j // n_q_blocks, 0, 0))]
    args = [a, w, g3]
    if rope_half:
        tab_spec = pl.BlockSpec((bm, 128), lambda i, j: (i % nsb, 0))
        in_specs += [tab_spec, tab_spec, tab_spec]
        args += list(tabs)
    return pl.pallas_call(
        functools.partial(_heads_kernel, heads=bn // 128, n_norm=2 * n_q_blocks, rope_half=rope_half),
        grid=(m // bm, n // bn),
        in_specs=in_specs,
        out_specs=pl.BlockSpec((bm, bn), lambda i, j: (i, j)),
        out_shape=jax.ShapeDtypeStruct((m, n), BF16),
        compiler_params=_cparams(("parallel", "parallel")),
        name=name,
    )(*args)


def _flash_stages(q_ref, k_ref, v_ref, o_ref, s_w, s_r, p_w, p_r, al_w, al_r, m_ref, l_ref,
                  lfin_ref, acc_ref, *, total, nk, coef):
    t = pl.program_id(0)
    s_w[...] = _dot_nt(q_ref[...], k_ref[...])

    valid_b = jnp.logical_and(t >= 1, t <= total)
    ki_b = jnp.clip(t - 1, 0, total - 1) % nk
    m_old = m_ref[...]
    m_prev = jnp.where(ki_b == 0, -jnp.inf, m_old)
    m_new = jnp.maximum(m_prev, jnp.max(s_r[...], axis=-1, keepdims=True))
    alpha = jnp.exp2((m_prev - m_new) * coef)
    p = jnp.exp2((s_r[...] - m_new) * coef)
    l_old = l_ref[...]
    l_new = alpha * l_old + jnp.sum(p, axis=-1, keepdims=True)
    p_w[...] = p.astype(p_w.dtype)
    al_w[...] = alpha
    m_ref[...] = jnp.where(valid_b, m_new, m_old)
    l_ref[...] = jnp.where(valid_b, l_new, l_old)
    lfin_ref[...] = jnp.where(jnp.logical_and(valid_b, ki_b == nk - 1), l_new, lfin_ref[...])

    acc = al_r[...] * acc_ref[...] + _dot(p_r[...], v_ref[...])
    acc_ref[...] = acc
    ki_c = jnp.clip(t - 2, 0, total - 1) % nk

    @pl.when(jnp.logical_and(t >= 2, ki_c == nk - 1))
    def _():
        o_ref[...] = (acc / lfin_ref[...]).astype(o_ref.dtype)


def _flash_kernel(q_ref, k_ref, v_ref, o_ref, s0, s1, p0, p1, al0, al1, m_ref, l_ref, lfin_ref,
                  acc_ref, **kw):
    t = pl.program_id(0)
    s_scr, p_scr, al_scr = (s0, s1), (p0, p1), (al0, al1)

    @pl.when(t == 0)
    def _():
        for ref in (s0, s1, p0, p1, al0, al1, m_ref, l_ref, acc_ref):
            ref[...] = jnp.zeros(ref.shape, ref.dtype)
        lfin_ref[...] = jnp.ones(lfin_ref.shape, F32)

    for par in (0, 1):
        @pl.when(t % 2 == par)
        def _(par=par):
            _flash_stages(q_ref, k_ref, v_ref, o_ref, s_scr[par], s_scr[1 - par],
                          p_scr[1 - par], p_scr[par], al_scr[1 - par], al_scr[par],
                          m_ref, l_ref, lfin_ref, acc_ref, **kw)


def _mla_attention(q, k, v, batch, seq, tq=1024, tk=2048):
    m = q.shape[0]
    tk = min(tk, seq // 2)
    nq, nk = seq // tq, seq // tk
    total = batch * MLA_HEADS * nq * nk

    def unravel(tt):
        ki = tt % nk
        r = tt // nk
        qi = r % nq
        r = r // nq
        return r // MLA_HEADS, r % MLA_HEADS, qi, ki

    def q_map(t):
        b, h, qi, _ = unravel(jnp.minimum(t, total - 1))
        return (b * nq + qi, h)

    def k_map(t):
        b, h, _, ki = unravel(jnp.minimum(t, total - 1))
        return (b * nk + ki, h)

    def v_map(t):
        b, h, _, ki = unravel(jnp.clip(t - 2, 0, total - 1))
        return (b * nk + ki, h)

    def o_map(t):
        b, h, qi, _ = unravel(jnp.clip(t - 2, 0, total - 1))
        return (b * nq + qi, h)

    coef = (MLA_QK ** -0.5) * float(np.log2(np.e))
    return pl.pallas_call(
        functools.partial(_flash_kernel, total=total, nk=nk, coef=coef),
        grid=(total + 2,),
        in_specs=[pl.BlockSpec((tq, MLA_HEAD_PAD), q_map),
                  pl.BlockSpec((tk, MLA_HEAD_PAD), k_map),
                  pl.BlockSpec((tk, MLA_V), v_map)],
        out_specs=pl.BlockSpec((tq, MLA_V), o_map),
        out_shape=jax.ShapeDtypeStruct((m, MLA_HEADS * MLA_V), BF16),
        scratch_shapes=[pltpu.VMEM((tq, tk), F32), pltpu.VMEM((tq, tk), F32),
                        pltpu.VMEM((tq, tk), BF16), pltpu.VMEM((tq, tk), BF16),
                        pltpu.VMEM((tq, 1), F32), pltpu.VMEM((tq, 1), F32),
                        pltpu.VMEM((tq, 1), F32), pltpu.VMEM((tq, 1), F32), pltpu.VMEM((tq, 1), F32),
                        pltpu.VMEM((tq, MLA_V), F32)],
        compiler_params=_cparams(("arbitrary",)),
        name="mla_attention",
    )(q, k, v)


DIL_SQ = 256
DIL_SK = DIL_SQ + 2 * DIL_RADIUS


def _dil_kernel(q_ref, kp_ref, k_ref, kn_ref, vp_ref, v_ref, vn_ref, o_ref, st_ref,
                kwin_ref, vwin_ref, *, tq, length):
    r = DIL_RADIUS
    q0 = pl.program_id(2) * tq
    kwin_ref[0:r] = kp_ref[...]
    kwin_ref[r:r + tq] = k_ref[...]
    kwin_ref[r + tq:] = kn_ref[...]
    vwin_ref[0:r] = vp_ref[...]
    vwin_ref[r:r + tq] = v_ref[...]
    vwin_ref[r + tq:] = vn_ref[...]
    scale = HEAD_DIM ** -0.5
    qi = lax.broadcasted_iota(jnp.int32, (DIL_SQ, DIL_SK), 0)
    kj = lax.broadcasted_iota(jnp.int32, (DIL_SQ, DIL_SK), 1)
    band = (kj >= qi) & (kj <= qi + 2 * r)
    lane = lax.broadcasted_iota(jnp.int32, (1, 128), 1)
    for c in range(tq // DIL_SQ):
        kpos = q0 + c * DIL_SQ - r + kj
        valid = band & (kpos >= 0) & (kpos < length)
        st = jnp.zeros((DIL_SQ, 128), F32)
        for h in range(DIL_HPG):
            hs = slice(h * 128, (h + 1) * 128)
            qh = q_ref[c * DIL_SQ:(c + 1) * DIL_SQ, hs]
            kh = kwin_ref[c * DIL_SQ:c * DIL_SQ + DIL_SK, hs]
            vh = vwin_ref[c * DIL_SQ:c * DIL_SQ + DIL_SK, hs]
            s = jnp.where(valid, _dot_nt(qh, kh) * scale, NEG_INF)
            mx = jnp.max(s, axis=-1, keepdims=True)
            p = jnp.exp(s - mx)
            den = jnp.sum(p, axis=-1, keepdims=True)
            o = _dot(p.astype(vh.dtype), vh) / den
            o_ref[c * DIL_SQ:(c + 1) * DIL_SQ, hs] = o
            st = jnp.where(lane == h, mx, st)
            st = jnp.where(lane == DIL_HPG + h, den, st)
        st_ref[c * DIL_SQ:(c + 1) * DIL_SQ, :] = st


def _dil_proj_kernel(a_ref, w_ref, g_ref, c_ref, sa_ref, sb_ref, o_ref, y_scr, *, dil, heads):
    z = _dot(a_ref[...], w_ref[...])
    part = pl.program_id(1)
    rows = z.shape[0] // dil

    @pl.when(part < 2)
    def _():
        g = g_ref[0]
        for h in range(heads):
            zh = z[:, h * 128:(h + 1) * 128]
            y = zh * lax.rsqrt(jnp.mean(zh * zh, axis=-1, keepdims=True) + NORM_EPS) * g
            y_scr[h] = _apply_rope(y, c_ref[...], sa_ref[...], sb_ref[...], DIL_ROT // 2)

    @pl.when(part == 2)
    def _():
        for h in range(heads):
            y_scr[h] = z[:, h * 128:(h + 1) * 128]

    for h in range(heads):
        hs = slice(h * 128, (h + 1) * 128)
        if dil == 1:
            o_ref[0, :, hs] = y_scr[h].astype(o_ref.dtype)
        else:
            for r in range(dil):
                o_ref[r, :, hs] = y_scr[h, pl.ds(r, rows, stride=dil), :].astype(o_ref.dtype)


def _dil_proj(h, w_b, gains, tabs, batch, seq, group, dil, bm=1024):
    m, k = h.shape
    w = DIL_HPG * 128
    nsb = seq // bm
    g3 = jnp.concatenate([gains, jnp.ones((1, 128), F32)], axis=0).reshape(3, 1, 128)
    tab_spec = pl.BlockSpec((bm, 128), lambda i, p: (i % nsb, 0))
    return pl.pallas_call(
        functools.partial(_dil_proj_kernel, dil=dil, heads=DIL_HPG),
        grid=(m // bm, 3),
        in_specs=[pl.BlockSpec((bm, k), lambda i, p: (i, 0)),
                  pl.BlockSpec((k, w), lambda i, p: (0, p * 3 + group)),
                  pl.BlockSpec((1, 1, 128), lambda i, p: (p, 0, 0)),
                  tab_spec, tab_spec, tab_spec],
        out_specs=pl.BlockSpec((None, dil, bm // dil, w), lambda i, p: (i // nsb, 0, i % nsb, p)),
        out_shape=jax.ShapeDtypeStruct((batch, dil, seq // dil, 3 * w), BF16),
        scratch_shapes=[pltpu.VMEM((DIL_HPG, bm, 128), F32)],
        compiler_params=_cparams(("parallel", "arbitrary")),
        name=f"dil_proj_g{group}",
    )(h, w_b, g3, *tabs)


def _dilated_group(qkv, group):
    batch, dil, length, _ = qkv.shape
    tq = min(512, length)
    r = DIL_RADIUS
    nq = length // tq
    hb = tq // r
    nhb = length // r
    w = DIL_HPG * 128

    def main(part):
        return pl.BlockSpec((None, None, tq, w), lambda b, rr, qi: (b, rr, qi, part))

    def prev(part):
        return pl.BlockSpec((None, None, r, w),
                            lambda b, rr, qi: (b, rr, jnp.maximum(qi * hb - 1, 0), part))

    def nxt(part):
        return pl.BlockSpec((None, None, r, w),
                            lambda b, rr, qi: (b, rr, jnp.minimum((qi + 1) * hb, nhb - 1), part))

    return pl.pallas_call(
        functools.partial(_dil_kernel, tq=tq, length=length),
        grid=(batch, dil, nq),
        in_specs=[main(0), prev(1), main(1), nxt(1), prev(2), main(2), nxt(2)],
        out_specs=[pl.BlockSpec((None, None, tq, w), lambda b, rr, qi: (b, rr, qi, 0)),
                   pl.BlockSpec((None, None, tq, 128), lambda b, rr, qi: (b, rr, qi, 0))],
        out_shape=[jax.ShapeDtypeStruct((batch, dil, length, w), F32),
                   jax.ShapeDtypeStruct((batch, dil, length, 128), F32)],
        scratch_shapes=[pltpu.VMEM((tq + 2 * r, w), BF16), pltpu.VMEM((tq + 2 * r, w), BF16)],
        compiler_params=_cparams(("parallel", "parallel", "parallel")),
        name=f"dilated_g{group}",
    )(qkv, qkv, qkv, qkv, qkv, qkv, qkv)


def _dil_merge_kernel(o0_ref, o1_ref, o2_ref, s0_ref, s1_ref, s2_ref, out_ref, on_scr, sn_scr,
                      *, dils):
    bm = out_ref.shape[0]
    for gi, (o_ref, s_ref) in enumerate(((o0_ref, s0_ref), (o1_ref, s1_ref), (o2_ref, s2_ref))):
        dil = dils[gi]
        for r in range(dil):
            rows = pl.ds(r, bm // dil, stride=dil) if dil > 1 else slice(None)
            sn_scr[gi, rows, :] = s_ref[r]
            for h in range(DIL_HPG):
                on_scr[gi, h, rows, :] = o_ref[r, :, h * 128:(h + 1) * 128]
    s0, s1, s2 = sn_scr[0], sn_scr[1], sn_scr[2]
    for h in range(DIL_HPG):
        hs = slice(h * 128, (h + 1) * 128)
        m0, m1, m2 = s0[:, h:h + 1], s1[:, h:h + 1], s2[:, h:h + 1]
        l0, l1, l2 = (s0[:, DIL_HPG + h:DIL_HPG + h + 1], s1[:, DIL_HPG + h:DIL_HPG + h + 1],
                      s2[:, DIL_HPG + h:DIL_HPG + h + 1])
        mx = jnp.maximum(jnp.maximum(m0, m1), m2)
        w0 = jnp.exp(m0 - mx) * l0
        w1 = jnp.exp(m1 - mx) * l1
        w2 = jnp.exp(m2 - mx) * l2
        num = w0 * on_scr[0, h] + w1 * on_scr[1, h] + w2 * on_scr[2, h]
        out_ref[:, hs] = (num / (w0 + w1 + w2)).astype(out_ref.dtype)


def _dil_merge(outs, stats, batch, seq, bm=512):
    w = DIL_HPG * 128
    nsb = seq // bm
    dils = tuple(o.shape[1] for o in outs)

    def spec(dil, width):
        return pl.BlockSpec((None, dil, bm // dil, width), lambda b, i: (b, 0, i, 0))

    return pl.pallas_call(
        functools.partial(_dil_merge_kernel, dils=dils),
        grid=(batch, nsb),
        in_specs=[spec(d, w) for d in dils] + [spec(d, 128) for d in dils],
        out_specs=pl.BlockSpec((bm, w), lambda b, i: (b * nsb + i, 0)),
        out_shape=jax.ShapeDtypeStruct((batch * seq, w), BF16),
        scratch_shapes=[pltpu.VMEM((3, DIL_HPG, bm, 128), F32), pltpu.VMEM((3, bm, 128), F32)],
        compiler_params=_cparams(("parallel", "parallel")),
        name="dilated_merge",
    )(*outs, *stats)


NA_TQ = NA_QROWS * GRID_W
NA_TK = NA_KROWS * GRID_W
NA_HPS = 8


def _na_bias_tables(rpb, rows):
    heads = rpb.shape[0]
    cols = np.arange(GRID_W)
    cs = np.clip(cols - NA_COLS // 2, 0, GRID_W - NA_COLS)
    kc = np.arange(GRID_W)
    col_ok = (kc[None, :] >= cs[:, None]) & (kc[None, :] < cs[:, None] + NA_COLS)
    col_idx = np.clip(kc[None, :] - cols[:, None] + (NA_COLS - 1), 0, 2 * NA_COLS - 2)
    toe = jnp.where(col_ok[None, None], rpb[:, :, col_idx], NEG_INF)
    tabs = []
    for i0, ks in ((0, 0), (NA_QROWS * 2, NA_QROWS), (rows - NA_QROWS, rows - NA_KROWS)):
        qi = i0 + np.arange(NA_QROWS)
        rs = np.clip(qi - NA_ROWS // 2, 0, rows - NA_ROWS)
        kr = ks + np.arange(NA_KROWS)
        row_ok = (kr[None, :] >= rs[:, None]) & (kr[None, :] < rs[:, None] + NA_ROWS)
        dr = np.clip(kr[None, :] - qi[:, None] + (NA_ROWS - 1), 0, 2 * NA_ROWS - 2)
        t = toe[:, dr]
        t = jnp.where(row_ok[None, :, :, None, None], t, NEG_INF)
        tabs.append(t.transpose(0, 1, 3, 2, 4).reshape(heads, NA_TQ, NA_TK))
    return jnp.stack(tabs)


def _na_kernel(q_ref, k0_ref, k1_ref, k2_ref, v0_ref, v1_ref, v2_ref, b_ref, o_ref):
    scale = HEAD_DIM ** -0.5
    for h in range(NA_HPS):
        hs = slice(h * 128, (h + 1) * 128)
        q = q_ref[0, :, hs]
        k = jnp.concatenate([k0_ref[0, :, hs], k1_ref[0, :, hs], k2_ref[0, :, hs]], axis=0)
        v = jnp.concatenate([v0_ref[0, :, hs], v1_ref[0, :, hs], v2_ref[0, :, hs]], axis=0)
        s = _dot_nt(q, k) * scale + b_ref[0, h]
        mx = jnp.max(s, axis=-1, keepdims=True)
        p = jnp.exp(s - mx)
        den = jnp.sum(p, axis=-1, keepdims=True)
        o = _dot(p.astype(v.dtype), v) / den
        o_ref[0, :, hs] = o.astype(o_ref.dtype)


def _na_attention(qkv, bias, batch, seq):
    rows = seq // GRID_W
    nblk = rows // NA_QROWS
    nkb = NA_KROWS // NA_QROWS
    ng = NA_HEADS // NA_HPS
    w = NA_HPS * 128
    qv = qkv.reshape(batch, seq, qkv.shape[1])

    def kv_spec(part, off):
        return pl.BlockSpec(
            (1, NA_TQ, w),
            lambda b, g, rb: (b, jnp.clip(rb - 1, 0, nblk - nkb) + off, part * ng + g))

    def bias_map(b, g, rb):
        cfg = jnp.where(rb == 0, 0, jnp.where(rb == nblk - 1, 2, 1))
        return (cfg, g, 0, 0)

    o = pl.pallas_call(
        _na_kernel,
        grid=(batch, ng, nblk),
        in_specs=[pl.BlockSpec((1, NA_TQ, w), lambda b, g, rb: (b, rb, g)),
                  kv_spec(1, 0), kv_spec(1, 1), kv_spec(1, 2),
                  kv_spec(2, 0), kv_spec(2, 1), kv_spec(2, 2),
                  pl.BlockSpec((1, NA_HPS, NA_TQ, NA_TK), bias_map)],
        out_specs=pl.BlockSpec((1, NA_TQ, w), lambda b, g, rb: (b, rb, g)),
        out_shape=jax.ShapeDtypeStruct((batch, seq, NA_HEADS * 128), BF16),
        compiler_params=_cparams(("parallel", "parallel", "arbitrary")),
        name="na_attention",
    )(qv, qv, qv, qv, qv, qv, qv, bias)
    return o.reshape(batch * seq, NA_HEADS * 128)


def _ffn_up_kernel(h_ref, w_ref, a_ref, ap_ref, an_ref, cw_ref, cb_ref, o_ref, *, bm, seq):
    i = pl.program_id(0)
    u = _dot(h_ref[...], w_ref[...])
    a = a_ref[...]
    first = (i * bm) % seq == 0
    last = ((i + 1) * bm) % seq == 0
    prev_row = jnp.where(first, 0.0, ap_ref[7:8, :])
    next_row = jnp.where(last, 0.0, an_ref[0:1, :])
    ridx = lax.broadcasted_iota(jnp.int32, (bm, 1), 0)
    a_prev = jnp.where(ridx == 0, prev_row, pltpu.roll(a, 1, 0))
    a_next = jnp.where(ridx == bm - 1, next_row, pltpu.roll(a, bm - 1, 0))
    cw = cw_ref[...]
    ac = a_prev * cw[0:1] + a * cw[1:2] + a_next * cw[2:3] + cb_ref[...]
    o_ref[...] = (ac * jax.nn.sigmoid(ac) * u).astype(o_ref.dtype)


def _ffn_up(h, w_up, a, conv_w, conv_b, seq, bm=1024, bn=512):
    m, k = h.shape
    n = w_up.shape[1]
    rb = bm // 8
    nrb = m // 8
    return pl.pallas_call(
        functools.partial(_ffn_up_kernel, bm=bm, seq=seq),
        grid=(m // bm, n // bn),
        in_specs=[pl.BlockSpec((bm, k), lambda i, j: (i, 0)),
                  pl.BlockSpec((k, bn), lambda i, j: (0, j)),
                  pl.BlockSpec((bm, bn), lambda i, j: (i, j)),
                  pl.BlockSpec((8, bn), lambda i, j: (jnp.maximum(i * rb - 1, 0), j)),
                  pl.BlockSpec((8, bn), lambda i, j: (jnp.minimum((i + 1) * rb, nrb - 1), j)),
                  pl.BlockSpec((3, bn), lambda i, j: (0, j)),
                  pl.BlockSpec((1, bn), lambda i, j: (0, j))],
        out_specs=pl.BlockSpec((bm, bn), lambda i, j: (i, j)),
        out_shape=jax.ShapeDtypeStruct((m, n), BF16),
        compiler_params=_cparams(("parallel", "parallel")),
        name="ffn_up",
    )(h, w_up, a, a, a, conv_w, conv_b)


def _conv_ffn(x2d, seq, norm_g, w_gate, w_up, conv_w, conv_b, w_down):
    pad = D_FF_PAD - D_FF
    d_model = w_gate.shape[0]
    wg = _cast_pad(w_gate, d_model, D_FF_PAD)
    wu = _cast_pad(w_up, d_model, D_FF_PAD)
    wd = _cast_pad(w_down, D_FF_PAD, d_model)
    cw = jnp.pad(conv_w, ((0, 0), (0, pad)))
    cb = jnp.pad(conv_b, ((0, pad),)).reshape(1, D_FF_PAD)
    h = _rmsnorm(x2d, norm_g)
    a = _mm(h, wg, F32, name="ffn_gate")
    g = _ffn_up(h, wu, a, cw, cb, seq)
    return _mm_res_acc(g, wd, x2d, bk=D_FF_PAD // 4, name="ffn_down")


def _even_layer(x2d, batch, seq, e_norm, w_in, q_lora_g, kv_lora_g, w_uq, w_ukv, mla_qn, mla_kn,
                dil_qn, dil_kn, w_out):
    c3 = MLA_Q_RANK + MLA_KV_RANK + MLA_ROPE
    w_lat = jnp.pad(w_in[:, :c3], ((0, 0), (0, LAT_PAD - c3))).astype(BF16)
    w_b = w_in[:, c3:].astype(BF16)
    hpad = MLA_HEAD_PAD - MLA_QK
    w_uq_p = jnp.pad(w_uq.reshape(MLA_Q_RANK, MLA_HEADS, MLA_QK), ((0, 0), (0, 0), (0, hpad)))
    w_uq_p = w_uq_p.reshape(MLA_Q_RANK, MLA_HEADS * MLA_HEAD_PAD).astype(BF16)
    gq_pad = jnp.pad(mla_qn, (0, hpad)).reshape(1, MLA_HEAD_PAD)
    gk_pad = jnp.pad(mla_kn, (0, hpad)).reshape(1, MLA_HEAD_PAD)
    mla_tabs = _rope_tables(seq, MLA_ROPE, 128)
    dil_tabs = _rope_tables(seq, DIL_ROT, 128)

    h = _rmsnorm(x2d, e_norm)
    cq, ckv, kpe = _lat_proj(h, w_lat, q_lora_g, kv_lora_g)
    q = _mla_q_proj(cq, w_uq_p, gq_pad, mla_tabs, seq)
    k, v = _mla_kv_proj(ckv, w_ukv.astype(BF16), kpe, gk_pad, mla_tabs, seq)
    o_a = _mla_attention(q, k, v, batch, seq)

    outs, stats = [], []
    for g, (window, dil) in enumerate(DIL_CONFIGS):
        assert window // (2 * dil) == DIL_RADIUS
        qkv_g = _dil_proj(h, w_b, jnp.stack([dil_qn, dil_kn]), dil_tabs, batch, seq, g, dil)
        o, st = _dilated_group(qkv_g, g)
        outs.append(o)
        stats.append(st)
    o_b = _dil_merge(outs, stats, batch, seq)
    return _mm_res2(o_a, o_b, _cast_pad(w_out, *w_out.shape), x2d, name="even_out_proj")


def _odd_layer(x2d, batch, seq, o_norm, w_qkv, qn, kn, rpb, w_out):
    h = _rmsnorm(x2d, o_norm)
    qkv = _heads_proj(h, _cast_pad(w_qkv, *w_qkv.shape), jnp.stack([qn, kn]), NA_HEADS * 128 // 1024,
                      None, seq, 0, name="na_qkv_proj")
    bias = _na_bias_tables(rpb, seq // GRID_W)
    o = _na_attention(qkv, bias, batch, seq)
    return _mm_res(o, _cast_pad(w_out, *w_out.shape), x2d, name="odd_out_proj")


def kernel(x, e_norm, e_w_in, e_q_lora_norm, e_kv_lora_norm, e_w_uq, e_w_ukv, e_mla_q_norm,
           e_mla_k_norm, e_dil_q_norm, e_dil_k_norm, e_w_out, o_norm, o_w_qkv, o_q_norm, o_k_norm,
           o_rpb, o_w_out, f_norm, f_w_gate, f_w_up, f_conv_w, f_conv_b, f_w_down):
    batch, seq, d = x.shape
    x2d = x.reshape(batch * seq, d)
    depth = f_norm.shape[0]
    for layer in range(depth):
        i = layer // 2
        if layer % 2 == 0:
            x2d = _even_layer(x2d, batch, seq, e_norm[i], e_w_in[i], e_q_lora_norm[i],
                              e_kv_lora_norm[i], e_w_uq[i], e_w_ukv[i], e_mla_q_norm[i],
                              e_mla_k_norm[i], e_dil_q_norm[i], e_dil_k_norm[i], e_w_out[i])
        else:
            x2d = _odd_layer(x2d, batch, seq, o_norm[i], o_w_qkv[i], o_q_norm[i], o_k_norm[i],
                             o_rpb[i], o_w_out[i])
        x2d = _conv_ffn(x2d, seq, f_norm[layer], f_w_gate[layer], f_w_up[layer], f_conv_w[layer],
                        f_conv_b[layer], f_w_down[layer])
    return x2d.reshape(batch, seq, d)
```

```python
import functools

import jax
import jax.numpy as jnp
import numpy as np
from jax import lax
from jax.experimental import pallas as pl
from jax.experimental.pallas import tpu as pltpu

F32 = jnp.float32
BF16 = jnp.bfloat16

HEAD_DIM = 128
ROPE_THETA = 500000.0
NORM_EPS = 1e-6
NEG_INF = -1e30

MLA_HEADS = 16
MLA_Q_RANK = 896
MLA_KV_RANK = 512
MLA_NOPE = 128
MLA_ROPE = 64
MLA_QK = MLA_NOPE + MLA_ROPE
MLA_V = 128
MLA_HEAD_PAD = 256
MLA_V_PAD = 256
LOG2E = float(np.log2(np.e))
LAT_PAD = 1536

DIL_CONFIGS = ((128, 1), (512, 4), (2048, 16))
DIL_HPG = 8
DIL_HEADS = 24
DIL_ROT = HEAD_DIM // 4
DIL_RADIUS = 64

GRID_W = 64
NA_HEADS = 32
NA_ROWS = 8
NA_COLS = 16
NA_QROWS = 4
NA_KROWS = 12

D_FF = 11008
D_FF_PAD = 11264

VMEM_LIMIT_BYTES = 56 * 1024 * 1024


def _cparams(sem):
    return pltpu.CompilerParams(dimension_semantics=sem, vmem_limit_bytes=VMEM_LIMIT_BYTES)


def _dot(a, b):
    return jnp.dot(a, b, preferred_element_type=F32)


def _dot_nt(a, b):
    return lax.dot_general(a, b, (((1,), (1,)), ((), ())), preferred_element_type=F32)


def _rmsnorm_kernel(x_ref, g_ref, o_ref):
    x = x_ref[...]
    y = x * lax.rsqrt(jnp.mean(x * x, axis=-1, keepdims=True) + NORM_EPS)
    o_ref[...] = (y * g_ref[...]).astype(o_ref.dtype)


def _rmsnorm(x2d, g, bm=512):
    m, d = x2d.shape
    return pl.pallas_call(
        _rmsnorm_kernel,
        grid=(m // bm,),
        in_specs=[pl.BlockSpec((bm, d), lambda i: (i, 0)),
                  pl.BlockSpec((1, d), lambda i: (0, 0))],
        out_specs=pl.BlockSpec((bm, d), lambda i: (i, 0)),
        out_shape=jax.ShapeDtypeStruct((m, d), BF16),
        compiler_params=_cparams(("parallel",)),
        name="rmsnorm",
    )(x2d, g.reshape(1, d))


def _cast_kernel(w_ref, o_ref, *, n_in_blocks, cols):
    i = pl.program_id(0)

    @pl.when(i < n_in_blocks)
    def _():
        o_ref[:, :cols] = w_ref[...].astype(o_ref.dtype)
        if o_ref.shape[1] > cols:
            o_ref[:, cols:] = jnp.zeros((o_ref.shape[0], o_ref.shape[1] - cols), o_ref.dtype)

    @pl.when(i >= n_in_blocks)
    def _():
        o_ref[...] = jnp.zeros(o_ref.shape, o_ref.dtype)


def _cast_pad(w, rows_out, cols_out, br=256):
    rows, cols = w.shape
    nin = rows // br
    return pl.pallas_call(
        functools.partial(_cast_kernel, n_in_blocks=nin, cols=cols),
        grid=(rows_out // br,),
        in_specs=[pl.BlockSpec((br, cols), lambda i: (jnp.minimum(i, nin - 1), 0))],
        out_specs=pl.BlockSpec((br, cols_out), lambda i: (i, 0)),
        out_shape=jax.ShapeDtypeStruct((rows_out, cols_out), BF16),
        compiler_params=_cparams(("parallel",)),
        name="cast_pad",
    )(w)


def _mm_kernel(a_ref, w_ref, o_ref):
    o_ref[...] = _dot(a_ref[...], w_ref[...]).astype(o_ref.dtype)


def _mm(a, w, out_dtype, bm=1024, bn=1024, name="mm"):
    m, k = a.shape
    n = w.shape[1]
    return pl.pallas_call(
        _mm_kernel,
        grid=(m // bm, n // bn),
        in_specs=[pl.BlockSpec((bm, k), lambda i, j: (i, 0)),
                  pl.BlockSpec((k, bn), lambda i, j: (0, j))],
        out_specs=pl.BlockSpec((bm, bn), lambda i, j: (i, j)),
        out_shape=jax.ShapeDtypeStruct((m, n), out_dtype),
        compiler_params=_cparams(("parallel", "parallel")),
        name=name,
    )(a, w)


def _mm_res_kernel(a_ref, w_ref, r_ref, o_ref):
    o_ref[...] = r_ref[...] + _dot(a_ref[...], w_ref[...])


def _mm_res(a, w, res, bm=1024, bn=512, name="mm_res"):
    m, k = a.shape
    n = w.shape[1]
    return pl.pallas_call(
        _mm_res_kernel,
        grid=(m // bm, n // bn),
        in_specs=[pl.BlockSpec((bm, k), lambda i, j: (i, 0)),
                  pl.BlockSpec((k, bn), lambda i, j: (0, j)),
                  pl.BlockSpec((bm, bn), lambda i, j: (i, j))],
        out_specs=pl.BlockSpec((bm, bn), lambda i, j: (i, j)),
        out_shape=jax.ShapeDtypeStruct((m, n), F32),
        compiler_params=_cparams(("parallel", "parallel")),
        name=name,
    )(a, w, res)


def _mm_res2_kernel(a1_ref, a2_ref, w1_ref, w2_ref, r_ref, o_ref):
    o_ref[...] = r_ref[...] + _dot(a1_ref[...], w1_ref[...]) + _dot(a2_ref[...], w2_ref[...])


def _mm_res2(a1, a2, w, res, bm=1024, bn=512, name="mm_res2"):
    m, k1 = a1.shape
    k2 = a2.shape[1]
    n = w.shape[1]
    assert k1 % k2 == 0 and w.shape[0] == k1 + k2
    return pl.pallas_call(
        _mm_res2_kernel,
        grid=(m // bm, n // bn),
        in_specs=[pl.BlockSpec((bm, k1), lambda i, j: (i, 0)),
                  pl.BlockSpec((bm, k2), lambda i, j: (i, 0)),
                  pl.BlockSpec((k1, bn), lambda i, j: (0, j)),
                  pl.BlockSpec((k2, bn), lambda i, j: (k1 // k2, j)),
                  pl.BlockSpec((bm, bn), lambda i, j: (i, j))],
        out_specs=pl.BlockSpec((bm, bn), lambda i, j: (i, j)),
        out_shape=jax.ShapeDtypeStruct((m, n), F32),
        compiler_params=_cparams(("parallel", "parallel")),
        name=name,
    )(a1, a2, w, w, res)


def _mm_res_acc_kernel(a_ref, w_ref, r_ref, o_ref, acc_ref):
    k = pl.program_id(2)

    @pl.when(k == 0)
    def _():
        acc_ref[...] = jnp.zeros(acc_ref.shape, F32)

    acc_ref[...] += _dot(a_ref[...], w_ref[...])

    @pl.when(k == pl.num_programs(2) - 1)
    def _():
        o_ref[...] = r_ref[...] + acc_ref[...]


def _mm_res_acc(a, w, res, bk, bm=1024, bn=1024, name="mm_res_acc"):
    m, k = a.shape
    n = w.shape[1]
    return pl.pallas_call(
        _mm_res_acc_kernel,
        grid=(m // bm, n // bn, k // bk),
        in_specs=[pl.BlockSpec((bm, bk), lambda i, j, kk: (i, kk)),
                  pl.BlockSpec((bk, bn), lambda i, j, kk: (kk, j)),
                  pl.BlockSpec((bm, bn), lambda i, j, kk: (i, j))],
        out_specs=pl.BlockSpec((bm, bn), lambda i, j, kk: (i, j)),
        out_shape=jax.ShapeDtypeStruct((m, n), F32),
        scratch_shapes=[pltpu.VMEM((bm, bn), F32)],
        compiler_params=_cparams(("parallel", "parallel", "arbitrary")),
        name=name,
    )(a, w, res)


def _rope_tables(seq, rot, width):
    half = rot // 2
    inv_freq = 1.0 / (ROPE_THETA ** (jnp.arange(half, dtype=F32) * (2.0 / rot)))
    ang = jnp.arange(seq, dtype=jnp.int32).astype(F32)[:, None] * inv_freq[None, :]
    cos, sin = jnp.cos(ang), jnp.sin(ang)
    zh = jnp.zeros((seq, half), F32)
    rest = width - rot
    c = jnp.concatenate([cos, cos, jnp.ones((seq, rest), F32)], axis=1)
    sa = jnp.concatenate([zh, sin, jnp.zeros((seq, rest), F32)], axis=1)
    sb = jnp.concatenate([-sin, zh, jnp.zeros((seq, rest), F32)], axis=1)
    return c, sa, sb


def _apply_rope(y, c, sa, sb, half):
    width = y.shape[-1]
    return y * c + pltpu.roll(y, half, 1) * sa + pltpu.roll(y, width - half, 1) * sb


def _lat_kernel(a_ref, w_ref, gq_ref, gkv_ref, cq_ref, ckv_ref, kpe_ref):
    z = _dot(a_ref[...], w_ref[...])
    c1 = MLA_Q_RANK
    c2 = c1 + MLA_KV_RANK
    cq = z[:, :c1]
    ckv = z[:, c1:c2]
    cq = cq * lax.rsqrt(jnp.mean(cq * cq, axis=-1, keepdims=True) + NORM_EPS)
    ckv = ckv * lax.rsqrt(jnp.mean(ckv * ckv, axis=-1, keepdims=True) + NORM_EPS)
    cq_ref[...] = (cq * gq_ref[...]).astype(cq_ref.dtype)
    ckv_ref[...] = (ckv * gkv_ref[...]).astype(ckv_ref.dtype)
    kpe_ref[...] = z[:, c2:]


def _lat_proj(h, w_lat, gq, gkv, bm=512):
    m, k = h.shape
    return pl.pallas_call(
        _lat_kernel,
        grid=(m // bm,),
        in_specs=[pl.BlockSpec((bm, k), lambda i: (i, 0)),
                  pl.BlockSpec((k, LAT_PAD), lambda i: (0, 0)),
                  pl.BlockSpec((1, MLA_Q_RANK), lambda i: (0, 0)),
                  pl.BlockSpec((1, MLA_KV_RANK), lambda i: (0, 0))],
        out_specs=[pl.BlockSpec((bm, MLA_Q_RANK), lambda i: (i, 0)),
                   pl.BlockSpec((bm, MLA_KV_RANK), lambda i: (i, 0)),
                   pl.BlockSpec((bm, 128), lambda i: (i, 0))],
        out_shape=[jax.ShapeDtypeStruct((m, MLA_Q_RANK), BF16),
                   jax.ShapeDtypeStruct((m, MLA_KV_RANK), BF16),
                   jax.ShapeDtypeStruct((m, 128), F32)],
        compiler_params=_cparams(("parallel",)),
        name="lat_proj",
    )(h, w_lat, gq.reshape(1, -1), gkv.reshape(1, -1))


def _mla_q_kernel(a_ref, w_ref, g_ref, c_ref, sa_ref, sb_ref, o_ref, *, heads):
    z = _dot(a_ref[...], w_ref[...])
    g = g_ref[...]
    c, sa, sb = c_ref[...], sa_ref[...], sb_ref[...]
    for h in range(heads):
        zh = z[:, h * MLA_HEAD_PAD:(h + 1) * MLA_HEAD_PAD]
        ms = jnp.sum(zh * zh, axis=-1, keepdims=True) * (1.0 / MLA_QK)
        y = zh * lax.rsqrt(ms + NORM_EPS) * g
        o_ref[:, h * MLA_HEAD_PAD:h * MLA_HEAD_PAD + 128] = y[:, :128].astype(o_ref.dtype)
        yr = _apply_rope(y[:, 128:], c, sa, sb, MLA_ROPE // 2)
        o_ref[:, h * MLA_HEAD_PAD + 128:(h + 1) * MLA_HEAD_PAD] = yr.astype(o_ref.dtype)


def _mla_q_proj(cq, w_uq, g_pad, tabs, seq, bm=1024, heads=4):
    m, k = cq.shape
    n = w_uq.shape[1]
    bn = heads * MLA_HEAD_PAD
    nsb = seq // bm
    tab_spec = pl.BlockSpec((bm, 128), lambda i, j: (i % nsb, 0))
    return pl.pallas_call(
        functools.partial(_mla_q_kernel, heads=heads),
        grid=(m // bm, n // bn),
        in_specs=[pl.BlockSpec((bm, k), lambda i, j: (i, 0)),
                  pl.BlockSpec((k, bn), lambda i, j: (0, j)),
                  pl.BlockSpec((1, MLA_HEAD_PAD), lambda i, j: (0, 0)),
                  tab_spec, tab_spec, tab_spec],
        out_specs=pl.BlockSpec((bm, bn), lambda i, j: (i, j)),
        out_shape=jax.ShapeDtypeStruct((m, n), BF16),
        compiler_params=_cparams(("parallel", "parallel")),
        name="mla_q_proj",
    )(cq, w_uq, g_pad, *tabs)


def _mla_kv_kernel(a_ref, w_ref, kpe_ref, g_ref, c_ref, sa_ref, sb_ref, k_ref, v_ref, *, heads):
    z = _dot(a_ref[...], w_ref[...])
    kpe = kpe_ref[...]
    g = g_ref[...]
    g_nope, g_rope = g[:, :128], g[:, 128:]
    c, sa, sb = c_ref[...], sa_ref[...], sb_ref[...]
    ss_pe = jnp.sum(kpe * kpe, axis=-1, keepdims=True)
    one_col = (lax.broadcasted_iota(jnp.int32, (1, 128), 1) == 0).astype(v_ref.dtype)
    for h in range(heads):
        nope = z[:, h * 256:h * 256 + 128]
        ms = (jnp.sum(nope * nope, axis=-1, keepdims=True) + ss_pe) * (1.0 / MLA_QK)
        r = lax.rsqrt(ms + NORM_EPS)
        k_ref[:, h * MLA_HEAD_PAD:h * MLA_HEAD_PAD + 128] = (nope * r * g_nope).astype(k_ref.dtype)
        kr = _apply_rope(kpe * r * g_rope, c, sa, sb, MLA_ROPE // 2)
        k_ref[:, h * MLA_HEAD_PAD + 128:(h + 1) * MLA_HEAD_PAD] = kr.astype(k_ref.dtype)
        v_ref[:, h * MLA_V_PAD:h * MLA_V_PAD + MLA_V] = z[:, h * 256 + 128:(h + 1) * 256].astype(v_ref.dtype)
        v_ref[:, h * MLA_V_PAD + MLA_V:(h + 1) * MLA_V_PAD] = jnp.broadcast_to(one_col, (z.shape[0], 128))


def _mla_kv_proj(ckv, w_ukv, kpe, g_pad, tabs, seq, bm=1024, heads=4):
    m, k = ckv.shape
    nsb = seq // bm
    tab_spec = pl.BlockSpec((bm, 128), lambda i, j: (i % nsb, 0))
    return pl.pallas_call(
        functools.partial(_mla_kv_kernel, heads=heads),
        grid=(m // bm, MLA_HEADS // heads),
        in_specs=[pl.BlockSpec((bm, k), lambda i, j: (i, 0)),
                  pl.BlockSpec((k, heads * 256), lambda i, j: (0, j)),
                  pl.BlockSpec((bm, 128), lambda i, j: (i, 0)),
                  pl.BlockSpec((1, MLA_HEAD_PAD), lambda i, j: (0, 0)),
                  tab_spec, tab_spec, tab_spec],
        out_specs=[pl.BlockSpec((bm, heads * MLA_HEAD_PAD), lambda i, j: (i, j)),
                   pl.BlockSpec((bm, heads * MLA_V_PAD), lambda i, j: (i, j))],
        out_shape=[jax.ShapeDtypeStruct((m, MLA_HEADS * MLA_HEAD_PAD), BF16),
                   jax.ShapeDtypeStruct((m, MLA_HEADS * MLA_V_PAD), BF16)],
        compiler_params=_cparams(("parallel", "parallel")),
        name="mla_kv_proj",
    )(ckv, w_ukv, kpe, g_pad, *tabs)


def _heads_kernel(a_ref, w_ref, g_ref, *rest, heads, n_norm, rope_half):
    if rope_half:
        c_ref, sa_ref, sb_ref, o_ref = rest
    else:
        (o_ref,) = rest
    z = _dot(a_ref[...], w_ref[...])
    j = pl.program_id(1)

    @pl.when(j < n_norm)
    def _():
        g = g_ref[0]
        for h in range(heads):
            zh = z[:, h * 128:(h + 1) * 128]
            y = zh * lax.rsqrt(jnp.mean(zh * zh, axis=-1, keepdims=True) + NORM_EPS) * g
            if rope_half:
                y = _apply_rope(y, c_ref[...], sa_ref[...], sb_ref[...], rope_half)
            o_ref[:, h * 128:(h + 1) * 128] = y.astype(o_ref.dtype)

    @pl.when(j >= n_norm)
    def _():
        o_ref[...] = z.astype(o_ref.dtype)


def _heads_proj(a, w, gains, n_q_blocks, tabs, seq, rope_half, bm=1024, bn=1024, name="heads_proj"):
    m, k = a.shape
    n = w.shape[1]
    nsb = seq // bm
    g3 = jnp.concatenate([gains, jnp.ones((1, 128), F32)], axis=0).reshape(3, 1, 128)
    in_specs = [pl.BlockSpec((bm, k), lambda i, j: (i, 0)),
                pl.BlockSpec((k, bn), lambda i, j: (0, j)),
                pl.BlockSpec((1, 1, 128), lambda i, j: (j // n_q_blocks, 0, 0))]
    args = [a, w, g3]
    if rope_half:
        tab_spec = pl.BlockSpec((bm, 128), lambda i, j: (i % nsb, 0))
        in_specs += [tab_spec, tab_spec, tab_spec]
        args += list(tabs)
    return pl.pallas_call(
        functools.partial(_heads_kernel, heads=bn // 128, n_norm=2 * n_q_blocks, rope_half=rope_half),
        grid=(m // bm, n // bn),
        in_specs=in_specs,
        out_specs=pl.BlockSpec((bm, bn), lambda i, j: (i, j)),
        out_shape=jax.ShapeDtypeStruct((m, n), BF16),
        compiler_params=_cparams(("parallel", "parallel")),
        name=name,
    )(*args)


def _flash_stages(q_ref, k_ref, v_ref, o_ref, s_w, s_r, p_w, p_r, al_w, al_r, mr_w, mr_r, m_ref,
                  acc_ref, *, total, nk):
    t = pl.program_id(0)
    s = _dot_nt(q_ref[...], k_ref[...])
    s_w[...] = s
    mr_w[...] = jnp.max(s, axis=-1, keepdims=True)

    valid_b = jnp.logical_and(t >= 1, t <= total)
    ki_b = jnp.clip(t - 1, 0, total - 1) % nk
    m_old = m_ref[...]
    m_prev = jnp.where(ki_b == 0, -jnp.inf, m_old)
    m_new = jnp.maximum(m_prev, mr_r[...])
    al_w[...] = jnp.exp2(m_prev - m_new)
    p_w[...] = jnp.exp2(s_r[...] - m_new).astype(p_w.dtype)
    m_ref[...] = jnp.where(valid_b, m_new, m_old)

    acc = al_r[...] * acc_ref[...] + _dot(p_r[...], v_ref[...])
    acc_ref[...] = acc
    ki_c = jnp.clip(t - 2, 0, total - 1) % nk

    @pl.when(jnp.logical_and(t >= 2, ki_c == nk - 1))
    def _():
        o_ref[...] = (acc[:, :MLA_V] / acc[:, MLA_V:MLA_V + 1]).astype(o_ref.dtype)


def _flash_kernel(q_ref, k_ref, v_ref, o_ref, s0, s1, p0, p1, al0, al1, mr0, mr1, m_ref, acc_ref, **kw):
    t = pl.program_id(0)
    s_scr, p_scr, al_scr, mr_scr = (s0, s1), (p0, p1), (al0, al1), (mr0, mr1)

    @pl.when(t == 0)
    def _():
        for ref in (s0, s1, p0, p1, al0, al1, mr0, mr1, m_ref, acc_ref):
            ref[...] = jnp.zeros(ref.shape, ref.dtype)

    for par in (0, 1):
        @pl.when(t % 2 == par)
        def _(par=par):
            _flash_stages(q_ref, k_ref, v_ref, o_ref, s_scr[par], s_scr[1 - par],
                          p_scr[1 - par], p_scr[par], al_scr[1 - par], al_scr[par],
                          mr_scr[par], mr_scr[1 - par], m_ref, acc_ref, **kw)


def _mla_attention(q, k, v, batch, seq, tq=1024, tk=2048):
    m = q.shape[0]
    tk = min(tk, seq // 2)
    nq, nk = seq // tq, seq // tk
    total = batch * MLA_HEADS * nq * nk

    def unravel(tt):
        ki = tt % nk
        r = tt // nk
        qi = r % nq
        r = r // nq
        return r // MLA_HEADS, r % MLA_HEADS, qi, ki

    def q_map(t):
        b, h, qi, _ = unravel(jnp.minimum(t, total - 1))
        return (b * nq + qi, h)

    def k_map(t):
        b, h, _, ki = unravel(jnp.minimum(t, total - 1))
        return (b * nk + ki, h)

    def v_map(t):
        b, h, _, ki = unravel(jnp.clip(t - 2, 0, total - 1))
        return (b * nk + ki, h)

    def o_map(t):
        b, h, qi, _ = unravel(jnp.clip(t - 2, 0, total - 1))
        return (b * nq + qi, h)

    def col():
        return pltpu.VMEM((tq, 1), F32)

    return pl.pallas_call(
        functools.partial(_flash_kernel, total=total, nk=nk),
        grid=(total + 2,),
        in_specs=[pl.BlockSpec((tq, MLA_HEAD_PAD), q_map),
                  pl.BlockSpec((tk, MLA_HEAD_PAD), k_map),
                  pl.BlockSpec((tk, MLA_V_PAD), v_map)],
        out_specs=pl.BlockSpec((tq, MLA_V), o_map),
        out_shape=jax.ShapeDtypeStruct((m, MLA_HEADS * MLA_V), BF16),
        scratch_shapes=[pltpu.VMEM((tq, tk), F32), pltpu.VMEM((tq, tk), F32),
                        pltpu.VMEM((tq, tk), BF16), pltpu.VMEM((tq, tk), BF16),
                        col(), col(), col(), col(), col(), pltpu.VMEM((tq, MLA_V_PAD), F32)],
        compiler_params=_cparams(("arbitrary",)),
        name="mla_attention",
    )(q, k, v)


DIL_SQ = 256
DIL_SK = DIL_SQ + 2 * DIL_RADIUS


def _dil_kernel(q_ref, kp_ref, k_ref, kn_ref, vp_ref, v_ref, vn_ref, o_ref, st_ref,
                kwin_ref, vwin_ref, *, tq, length):
    r = DIL_RADIUS
    q0 = pl.program_id(2) * tq
    kwin_ref[0:r] = kp_ref[...]
    kwin_ref[r:r + tq] = k_ref[...]
    kwin_ref[r + tq:] = kn_ref[...]
    vwin_ref[0:r] = vp_ref[...]
    vwin_ref[r:r + tq] = v_ref[...]
    vwin_ref[r + tq:] = vn_ref[...]
    qi = lax.broadcasted_iota(jnp.int32, (DIL_SQ, DIL_SK), 0)
    kj = lax.broadcasted_iota(jnp.int32, (DIL_SQ, DIL_SK), 1)
    band = (kj >= qi) & (kj <= qi + 2 * r)
    lane = lax.broadcasted_iota(jnp.int32, (1, 128), 1)
    for c in range(tq // DIL_SQ):
        kpos = q0 + c * DIL_SQ - r + kj
        valid = band & (kpos >= 0) & (kpos < length)
        st = jnp.zeros((DIL_SQ, 128), F32)
        for h in range(DIL_HPG):
            hs = slice(h * 128, (h + 1) * 128)
            qh = q_ref[c * DIL_SQ:(c + 1) * DIL_SQ, hs]
            kh = kwin_ref[c * DIL_SQ:c * DIL_SQ + DIL_SK, hs]
            vh = vwin_ref[c * DIL_SQ:c * DIL_SQ + DIL_SK, hs]
            s = jnp.where(valid, _dot_nt(qh, kh), NEG_INF)
            mx = jnp.max(s, axis=-1, keepdims=True)
            p = jnp.exp2(s - mx)
            den = jnp.sum(p, axis=-1, keepdims=True)
            o = _dot(p.astype(vh.dtype), vh) / den
            o_ref[c * DIL_SQ:(c + 1) * DIL_SQ, hs] = o
            st = jnp.where(lane == h, mx, st)
            st = jnp.where(lane == DIL_HPG + h, den, st)
        st_ref[c * DIL_SQ:(c + 1) * DIL_SQ, :] = st


def _dil_proj_kernel(a_ref, w_ref, g_ref, c_ref, sa_ref, sb_ref, o_ref, y_scr, *, dil, heads):
    z = _dot(a_ref[...], w_ref[...])
    part = pl.program_id(1)
    rows = z.shape[0] // dil

    @pl.when(part < 2)
    def _():
        g = g_ref[0]
        for h in range(heads):
            zh = z[:, h * 128:(h + 1) * 128]
            y = zh * lax.rsqrt(jnp.mean(zh * zh, axis=-1, keepdims=True) + NORM_EPS) * g
            y_scr[h] = _apply_rope(y, c_ref[...], sa_ref[...], sb_ref[...], DIL_ROT // 2)

    @pl.when(part == 2)
    def _():
        for h in range(heads):
            y_scr[h] = z[:, h * 128:(h + 1) * 128]

    for h in range(heads):
        hs = slice(h * 128, (h + 1) * 128)
        if dil == 1:
            o_ref[0, :, hs] = y_scr[h].astype(o_ref.dtype)
        else:
            for r in range(dil):
                o_ref[r, :, hs] = y_scr[h, pl.ds(r, rows, stride=dil), :].astype(o_ref.dtype)


def _dil_proj(h, w_b, gains, tabs, batch, seq, group, dil, bm=1024):
    m, k = h.shape
    w = DIL_HPG * 128
    nsb = seq // bm
    g3 = jnp.concatenate([gains, jnp.ones((1, 128), F32)], axis=0).reshape(3, 1, 128)
    tab_spec = pl.BlockSpec((bm, 128), lambda i, p: (i % nsb, 0))
    return pl.pallas_call(
        functools.partial(_dil_proj_kernel, dil=dil, heads=DIL_HPG),
        grid=(m // bm, 3),
        in_specs=[pl.BlockSpec((bm, k), lambda i, p: (i, 0)),
                  pl.BlockSpec((k, w), lambda i, p: (0, p * 3 + group)),
                  pl.BlockSpec((1, 1, 128), lambda i, p: (p, 0, 0)),
                  tab_spec, tab_spec, tab_spec],
        out_specs=pl.BlockSpec((None, dil, bm // dil, w), lambda i, p: (i // nsb, 0, i % nsb, p)),
        out_shape=jax.ShapeDtypeStruct((batch, dil, seq // dil, 3 * w), BF16),
        scratch_shapes=[pltpu.VMEM((DIL_HPG, bm, 128), F32)],
        compiler_params=_cparams(("parallel", "arbitrary")),
        name=f"dil_proj_g{group}",
    )(h, w_b, g3, *tabs)


def _dilated_group(qkv, group):
    batch, dil, length, _ = qkv.shape
    tq = min(512, length)
    r = DIL_RADIUS
    nq = length // tq
    hb = tq // r
    nhb = length // r
    w = DIL_HPG * 128

    def main(part):
        return pl.BlockSpec((None, None, tq, w), lambda b, rr, qi: (b, rr, qi, part))

    def prev(part):
        return pl.BlockSpec((None, None, r, w),
                            lambda b, rr, qi: (b, rr, jnp.maximum(qi * hb - 1, 0), part))

    def nxt(part):
        return pl.BlockSpec((None, None, r, w),
                            lambda b, rr, qi: (b, rr, jnp.minimum((qi + 1) * hb, nhb - 1), part))

    return pl.pallas_call(
        functools.partial(_dil_kernel, tq=tq, length=length),
        grid=(batch, dil, nq),
        in_specs=[main(0), prev(1), main(1), nxt(1), prev(2), main(2), nxt(2)],
        out_specs=[pl.BlockSpec((None, None, tq, w), lambda b, rr, qi: (b, rr, qi, 0)),
                   pl.BlockSpec((None, None, tq, 128), lambda b, rr, qi: (b, rr, qi, 0))],
        out_shape=[jax.ShapeDtypeStruct((batch, dil, length, w), F32),
                   jax.ShapeDtypeStruct((batch, dil, length, 128), F32)],
        scratch_shapes=[pltpu.VMEM((tq + 2 * r, w), BF16), pltpu.VMEM((tq + 2 * r, w), BF16)],
        compiler_params=_cparams(("parallel", "parallel", "parallel")),
        name=f"dilated_g{group}",
    )(qkv, qkv, qkv, qkv, qkv, qkv, qkv)


def _dil_merge_kernel(o0_ref, o1_ref, o2_ref, s0_ref, s1_ref, s2_ref, out_ref, on_scr, sn_scr,
                      *, dils):
    bm = out_ref.shape[0]
    for gi, (o_ref, s_ref) in enumerate(((o0_ref, s0_ref), (o1_ref, s1_ref), (o2_ref, s2_ref))):
        dil = dils[gi]
        for r in range(dil):
            rows = pl.ds(r, bm // dil, stride=dil) if dil > 1 else slice(None)
            sn_scr[gi, rows, :] = s_ref[r]
            for h in range(DIL_HPG):
                on_scr[gi, h, rows, :] = o_ref[r, :, h * 128:(h + 1) * 128]
    s0, s1, s2 = sn_scr[0], sn_scr[1], sn_scr[2]
    for h in range(DIL_HPG):
        hs = slice(h * 128, (h + 1) * 128)
        m0, m1, m2 = s0[:, h:h + 1], s1[:, h:h + 1], s2[:, h:h + 1]
        l0, l1, l2 = (s0[:, DIL_HPG + h:DIL_HPG + h + 1], s1[:, DIL_HPG + h:DIL_HPG + h + 1],
                      s2[:, DIL_HPG + h:DIL_HPG + h + 1])
        mx = jnp.maximum(jnp.maximum(m0, m1), m2)
        w0 = jnp.exp2(m0 - mx) * l0
        w1 = jnp.exp2(m1 - mx) * l1
        w2 = jnp.exp2(m2 - mx) * l2
        num = w0 * on_scr[0, h] + w1 * on_scr[1, h] + w2 * on_scr[2, h]
        out_ref[:, hs] = (num / (w0 + w1 + w2)).astype(out_ref.dtype)


def _dil_merge(outs, stats, batch, seq, bm=512):
    w = DIL_HPG * 128
    nsb = seq // bm
    dils = tuple(o.shape[1] for o in outs)

    def spec(dil, width):
        return pl.BlockSpec((None, dil, bm // dil, width), lambda b, i: (b, 0, i, 0))

    return pl.pallas_call(
        functools.partial(_dil_merge_kernel, dils=dils),
        grid=(batch, nsb),
        in_specs=[spec(d, w) for d in dils] + [spec(d, 128) for d in dils],
        out_specs=pl.BlockSpec((bm, w), lambda b, i: (b * nsb + i, 0)),
        out_shape=jax.ShapeDtypeStruct((batch * seq, w), BF16),
        scratch_shapes=[pltpu.VMEM((3, DIL_HPG, bm, 128), F32), pltpu.VMEM((3, bm, 128), F32)],
        compiler_params=_cparams(("parallel", "parallel")),
        name="dilated_merge",
    )(*outs, *stats)


NA_TQ = NA_QROWS * GRID_W
NA_TK = NA_KROWS * GRID_W
NA_HPS = 8


def _na_bias_tables(rpb, rows):
    heads = rpb.shape[0]
    cols = np.arange(GRID_W)
    cs = np.clip(cols - NA_COLS // 2, 0, GRID_W - NA_COLS)
    kc = np.arange(GRID_W)
    col_ok = (kc[None, :] >= cs[:, None]) & (kc[None, :] < cs[:, None] + NA_COLS)
    col_idx = np.clip(kc[None, :] - cols[:, None] + (NA_COLS - 1), 0, 2 * NA_COLS - 2)
    toe = jnp.where(col_ok[None, None], (rpb * LOG2E)[:, :, col_idx], NEG_INF)
    tabs = []
    for i0, ks in ((0, 0), (NA_QROWS * 2, NA_QROWS), (rows - NA_QROWS, rows - NA_KROWS)):
        qi = i0 + np.arange(NA_QROWS)
        rs = np.clip(qi - NA_ROWS // 2, 0, rows - NA_ROWS)
        kr = ks + np.arange(NA_KROWS)
        row_ok = (kr[None, :] >= rs[:, None]) & (kr[None, :] < rs[:, None] + NA_ROWS)
        dr = np.clip(kr[None, :] - qi[:, None] + (NA_ROWS - 1), 0, 2 * NA_ROWS - 2)
        t = toe[:, dr]
        t = jnp.where(row_ok[None, :, :, None, None], t, NEG_INF)
        tabs.append(t.transpose(0, 1, 3, 2, 4).reshape(heads, NA_TQ, NA_TK))
    return jnp.stack(tabs)


def _na_kernel(q_ref, k0_ref, k1_ref, k2_ref, v0_ref, v1_ref, v2_ref, b_ref, o_ref):
    for h in range(NA_HPS):
        hs = slice(h * 128, (h + 1) * 128)
        q = q_ref[0, :, hs]
        k = jnp.concatenate([k0_ref[0, :, hs], k1_ref[0, :, hs], k2_ref[0, :, hs]], axis=0)
        v = jnp.concatenate([v0_ref[0, :, hs], v1_ref[0, :, hs], v2_ref[0, :, hs]], axis=0)
        s = _dot_nt(q, k) + b_ref[0, h]
        mx = jnp.max(s, axis=-1, keepdims=True)
        p = jnp.exp2(s - mx)
        den = jnp.sum(p, axis=-1, keepdims=True)
        o = _dot(p.astype(v.dtype), v) / den
        o_ref[0, :, hs] = o.astype(o_ref.dtype)


def _na_attention(qkv, bias, batch, seq):
    rows = seq // GRID_W
    nblk = rows // NA_QROWS
    nkb = NA_KROWS // NA_QROWS
    ng = NA_HEADS // NA_HPS
    w = NA_HPS * 128
    qv = qkv.reshape(batch, seq, qkv.shape[1])

    def kv_spec(part, off):
        return pl.BlockSpec(
            (1, NA_TQ, w),
            lambda b, g, rb: (b, jnp.clip(rb - 1, 0, nblk - nkb) + off, part * ng + g))

    def bias_map(b, g, rb):
        cfg = jnp.where(rb == 0, 0, jnp.where(rb == nblk - 1, 2, 1))
        return (cfg, g, 0, 0)

    o = pl.pallas_call(
        _na_kernel,
        grid=(batch, ng, nblk),
        in_specs=[pl.BlockSpec((1, NA_TQ, w), lambda b, g, rb: (b, rb, g)),
                  kv_spec(1, 0), kv_spec(1, 1), kv_spec(1, 2),
                  kv_spec(2, 0), kv_spec(2, 1), kv_spec(2, 2),
                  pl.BlockSpec((1, NA_HPS, NA_TQ, NA_TK), bias_map)],
        out_specs=pl.BlockSpec((1, NA_TQ, w), lambda b, g, rb: (b, rb, g)),
        out_shape=jax.ShapeDtypeStruct((batch, seq, NA_HEADS * 128), BF16),
        compiler_params=_cparams(("parallel", "parallel", "arbitrary")),
        name="na_attention",
    )(qv, qv, qv, qv, qv, qv, qv, bias)
    return o.reshape(batch * seq, NA_HEADS * 128)


def _ffn_up_kernel(h_ref, w_ref, a_ref, ap_ref, an_ref, cw_ref, cb_ref, o_ref, *, bm, seq):
    i = pl.program_id(0)
    u = _dot(h_ref[...], w_ref[...])
    a = a_ref[...]
    first = (i * bm) % seq == 0
    last = ((i + 1) * bm) % seq == 0
    prev_row = jnp.where(first, 0.0, ap_ref[7:8, :])
    next_row = jnp.where(last, 0.0, an_ref[0:1, :])
    ridx = lax.broadcasted_iota(jnp.int32, (bm, 1), 0)
    a_prev = jnp.where(ridx == 0, prev_row, pltpu.roll(a, 1, 0))
    a_next = jnp.where(ridx == bm - 1, next_row, pltpu.roll(a, bm - 1, 0))
    cw = cw_ref[...]
    ac = a_prev * cw[0:1] + a * cw[1:2] + a_next * cw[2:3] + cb_ref[...]
    o_ref[...] = (ac * jax.nn.sigmoid(ac) * u).astype(o_ref.dtype)


def _ffn_up(h, w_up, a, conv_w, conv_b, seq, bm=1024, bn=512):
    m, k = h.shape
    n = w_up.shape[1]
    rb = bm // 8
    nrb = m // 8
    return pl.pallas_call(
        functools.partial(_ffn_up_kernel, bm=bm, seq=seq),
        grid=(m // bm, n // bn),
        in_specs=[pl.BlockSpec((bm, k), lambda i, j: (i, 0)),
                  pl.BlockSpec((k, bn), lambda i, j: (0, j)),
                  pl.BlockSpec((bm, bn), lambda i, j: (i, j)),
                  pl.BlockSpec((8, bn), lambda i, j: (jnp.maximum(i * rb - 1, 0), j)),
                  pl.BlockSpec((8, bn), lambda i, j: (jnp.minimum((i + 1) * rb, nrb - 1), j)),
                  pl.BlockSpec((3, bn), lambda i, j: (0, j)),
                  pl.BlockSpec((1, bn), lambda i, j: (0, j))],
        out_specs=pl.BlockSpec((bm, bn), lambda i, j: (i, j)),
        out_shape=jax.ShapeDtypeStruct((m, n), BF16),
        compiler_params=_cparams(("parallel", "parallel")),
        name="ffn_up",
    )(h, w_up, a, a, a, conv_w, conv_b)


def _conv_ffn(x2d, seq, norm_g, w_gate, w_up, conv_w, conv_b, w_down):
    pad = D_FF_PAD - D_FF
    d_model = w_gate.shape[0]
    wg = _cast_pad(w_gate, d_model, D_FF_PAD)
    wu = _cast_pad(w_up, d_model, D_FF_PAD)
    wd = _cast_pad(w_down, D_FF_PAD, d_model)
    cw = jnp.pad(conv_w, ((0, 0), (0, pad)))
    cb = jnp.pad(conv_b, ((0, pad),)).reshape(1, D_FF_PAD)
    h = _rmsnorm(x2d, norm_g)
    a = _mm(h, wg, F32, name="ffn_gate")
    g = _ffn_up(h, wu, a, cw, cb, seq)
    return _mm_res_acc(g, wd, x2d, bk=D_FF_PAD // 4, name="ffn_down")


def _even_layer(x2d, batch, seq, e_norm, w_in, q_lora_g, kv_lora_g, w_uq, w_ukv, mla_qn, mla_kn,
                dil_qn, dil_kn, w_out):
    c3 = MLA_Q_RANK + MLA_KV_RANK + MLA_ROPE
    w_lat = jnp.pad(w_in[:, :c3], ((0, 0), (0, LAT_PAD - c3))).astype(BF16)
    w_b = lax.optimization_barrier(w_in[:, c3:].astype(BF16))
    hpad = MLA_HEAD_PAD - MLA_QK
    w_uq_p = jnp.pad(w_uq.reshape(MLA_Q_RANK, MLA_HEADS, MLA_QK), ((0, 0), (0, 0), (0, hpad)))
    w_uq_p = w_uq_p.reshape(MLA_Q_RANK, MLA_HEADS * MLA_HEAD_PAD).astype(BF16)
    gq_pad = jnp.pad(mla_qn * (MLA_QK ** -0.5 * LOG2E), (0, hpad)).reshape(1, MLA_HEAD_PAD)
    gk_pad = jnp.pad(mla_kn, (0, hpad)).reshape(1, MLA_HEAD_PAD)
    mla_tabs = _rope_tables(seq, MLA_ROPE, 128)
    dil_tabs = _rope_tables(seq, DIL_ROT, 128)

    h = _rmsnorm(x2d, e_norm)
    cq, ckv, kpe = _lat_proj(h, w_lat, q_lora_g, kv_lora_g)
    q = _mla_q_proj(cq, w_uq_p, gq_pad, mla_tabs, seq)
    k, v = _mla_kv_proj(ckv, w_ukv.astype(BF16), kpe, gk_pad, mla_tabs, seq)
    o_a = _mla_attention(q, k, v, batch, seq)

    outs, stats = [], []
    for g, (window, dil) in enumerate(DIL_CONFIGS):
        assert window // (2 * dil) == DIL_RADIUS
        qkv_g = _dil_proj(h, w_b, jnp.stack([dil_qn * (HEAD_DIM ** -0.5 * LOG2E), dil_kn]), dil_tabs,
                          batch, seq, g, dil)
        o, st = _dilated_group(qkv_g, g)
        outs.append(o)
        stats.append(st)
    o_b = _dil_merge(outs, stats, batch, seq)
    return _mm_res2(o_a, o_b, _cast_pad(w_out, *w_out.shape), x2d, name="even_out_proj")


def _odd_layer(x2d, batch, seq, o_norm, w_qkv, qn, kn, rpb, w_out):
    h = _rmsnorm(x2d, o_norm)
    qkv = _heads_proj(h, _cast_pad(w_qkv, *w_qkv.shape),
                      jnp.stack([qn * (HEAD_DIM ** -0.5 * LOG2E), kn]), NA_HEADS * 128 // 1024,
                      None, seq, 0, name="na_qkv_proj")
    bias = _na_bias_tables(rpb, seq // GRID_W)
    o = _na_attention(qkv, bias, batch, seq)
    return _mm_res(o, _cast_pad(w_out, *w_out.shape), x2d, name="odd_out_proj")


def kernel(x, e_norm, e_w_in, e_q_lora_norm, e_kv_lora_norm, e_w_uq, e_w_ukv, e_mla_q_norm,
           e_mla_k_norm, e_dil_q_norm, e_dil_k_norm, e_w_out, o_norm, o_w_qkv, o_q_norm, o_k_norm,
           o_rpb, o_w_out, f_norm, f_w_gate, f_w_up, f_conv_w, f_conv_b, f_w_down):
    batch, seq, d = x.shape
    x2d = x.reshape(batch * seq, d)
    depth = f_norm.shape[0]
    for layer in range(depth):
        i = layer // 2
        if layer % 2 == 0:
            x2d = _even_layer(x2d, batch, seq, e_norm[i], e_w_in[i], e_q_lora_norm[i],
                              e_kv_lora_norm[i], e_w_uq[i], e_w_ukv[i], e_mla_q_norm[i],
                              e_mla_k_norm[i], e_dil_q_norm[i], e_dil_k_norm[i], e_w_out[i])
        else:
            x2d = _odd_layer(x2d, batch, seq, o_norm[i], o_w_qkv[i], o_q_norm[i], o_k_norm[i],
                             o_rpb[i], o_w_out[i])
        x2d = _conv_ffn(x2d, seq, f_norm[layer], f_w_gate[layer], f_w_up[layer], f_conv_w[layer],
                        f_conv_b[layer], f_w_down[layer])
    return x2d.reshape(batch, seq, d)
```

```python
import functools

import jax
import jax.numpy as jnp
import numpy as np
from jax import lax
from jax.experimental import pallas as pl
from jax.experimental.pallas import tpu as pltpu

F32 = jnp.float32
BF16 = jnp.bfloat16

HEAD_DIM = 128
ROPE_THETA = 500000.0
NORM_EPS = 1e-6
NEG_INF = -1e30

MLA_HEADS = 16
MLA_Q_RANK = 896
MLA_KV_RANK = 512
MLA_NOPE = 128
MLA_ROPE = 64
MLA_QK = MLA_NOPE + MLA_ROPE
MLA_V = 128
MLA_HEAD_PAD = 256
MLA_V_PAD = 256
LOG2E = float(np.log2(np.e))
LAT_PAD = 1536

DIL_CONFIGS = ((128, 1), (512, 4), (2048, 16))
DIL_HPG = 8
DIL_HEADS = 24
DIL_ROT = HEAD_DIM // 4
DIL_RADIUS = 64

GRID_W = 64
NA_HEADS = 32
NA_ROWS = 8
NA_COLS = 16
NA_QROWS = 4
NA_KROWS = 12

D_FF = 11008
D_FF_PAD = 11264

VMEM_LIMIT_BYTES = 56 * 1024 * 1024


def _cparams(sem):
    return pltpu.CompilerParams(dimension_semantics=sem, vmem_limit_bytes=VMEM_LIMIT_BYTES)


def _dot(a, b):
    return jnp.dot(a, b, preferred_element_type=F32)


def _dot_nt(a, b):
    return lax.dot_general(a, b, (((1,), (1,)), ((), ())), preferred_element_type=F32)


def _rmsnorm_kernel(x_ref, g_ref, o_ref):
    x = x_ref[...]
    y = x * lax.rsqrt(jnp.mean(x * x, axis=-1, keepdims=True) + NORM_EPS)
    o_ref[...] = (y * g_ref[...]).astype(o_ref.dtype)


def _rmsnorm(x2d, g, bm=512):
    m, d = x2d.shape
    return pl.pallas_call(
        _rmsnorm_kernel,
        grid=(m // bm,),
        in_specs=[pl.BlockSpec((bm, d), lambda i: (i, 0)),
                  pl.BlockSpec((1, d), lambda i: (0, 0))],
        out_specs=pl.BlockSpec((bm, d), lambda i: (i, 0)),
        out_shape=jax.ShapeDtypeStruct((m, d), BF16),
        compiler_params=_cparams(("parallel",)),
        name="rmsnorm",
    )(x2d, g.reshape(1, d))


def _cast_kernel(w_ref, o_ref, *, n_in_blocks, cols):
    i = pl.program_id(0)

    @pl.when(i < n_in_blocks)
    def _():
        o_ref[:, :cols] = w_ref[...].astype(o_ref.dtype)
        if o_ref.shape[1] > cols:
            o_ref[:, cols:] = jnp.zeros((o_ref.shape[0], o_ref.shape[1] - cols), o_ref.dtype)

    @pl.when(i >= n_in_blocks)
    def _():
        o_ref[...] = jnp.zeros(o_ref.shape, o_ref.dtype)


def _cast_pad(w_stack, layer, rows_out=None, cols_out=None, br=256):
    _, rows, cols = w_stack.shape
    rows_out = rows_out or rows
    cols_out = cols_out or cols
    nin = rows // br
    return pl.pallas_call(
        functools.partial(_cast_kernel, n_in_blocks=nin, cols=cols),
        grid=(rows_out // br,),
        in_specs=[pl.BlockSpec((None, br, cols), lambda i: (layer, jnp.minimum(i, nin - 1), 0))],
        out_specs=pl.BlockSpec((br, cols_out), lambda i: (i, 0)),
        out_shape=jax.ShapeDtypeStruct((rows_out, cols_out), BF16),
        compiler_params=_cparams(("parallel",)),
        name="cast_pad",
    )(w_stack)


def _mm_kernel(a_ref, w_ref, o_ref):
    o_ref[...] = _dot(a_ref[...], w_ref[...]).astype(o_ref.dtype)


def _mm(a, w, out_dtype, bm=1024, bn=1024, name="mm"):
    m, k = a.shape
    n = w.shape[1]
    return pl.pallas_call(
        _mm_kernel,
        grid=(m // bm, n // bn),
        in_specs=[pl.BlockSpec((bm, k), lambda i, j: (i, 0)),
                  pl.BlockSpec((k, bn), lambda i, j: (0, j))],
        out_specs=pl.BlockSpec((bm, bn), lambda i, j: (i, j)),
        out_shape=jax.ShapeDtypeStruct((m, n), out_dtype),
        compiler_params=_cparams(("parallel", "parallel")),
        name=name,
    )(a, w)


def _mm_res_kernel(a_ref, w_ref, r_ref, o_ref):
    o_ref[...] = r_ref[...] + _dot(a_ref[...], w_ref[...])


def _mm_res(a, w, res, bm=1024, bn=512, name="mm_res"):
    m, k = a.shape
    n = w.shape[1]
    return pl.pallas_call(
        _mm_res_kernel,
        grid=(m // bm, n // bn),
        in_specs=[pl.BlockSpec((bm, k), lambda i, j: (i, 0)),
                  pl.BlockSpec((k, bn), lambda i, j: (0, j)),
                  pl.BlockSpec((bm, bn), lambda i, j: (i, j))],
        out_specs=pl.BlockSpec((bm, bn), lambda i, j: (i, j)),
        out_shape=jax.ShapeDtypeStruct((m, n), F32),
        compiler_params=_cparams(("parallel", "parallel")),
        name=name,
    )(a, w, res)


def _mm_res2_kernel(a1_ref, a2_ref, w1_ref, w2_ref, r_ref, o_ref):
    o_ref[...] = r_ref[...] + _dot(a1_ref[...], w1_ref[...]) + _dot(a2_ref[...], w2_ref[...])


def _mm_res2(a1, a2, w, res, bm=1024, bn=512, name="mm_res2"):
    m, k1 = a1.shape
    k2 = a2.shape[1]
    n = w.shape[1]
    assert k1 % k2 == 0 and w.shape[0] == k1 + k2
    return pl.pallas_call(
        _mm_res2_kernel,
        grid=(m // bm, n // bn),
        in_specs=[pl.BlockSpec((bm, k1), lambda i, j: (i, 0)),
                  pl.BlockSpec((bm, k2), lambda i, j: (i, 0)),
                  pl.BlockSpec((k1, bn), lambda i, j: (0, j)),
                  pl.BlockSpec((k2, bn), lambda i, j: (k1 // k2, j)),
                  pl.BlockSpec((bm, bn), lambda i, j: (i, j))],
        out_specs=pl.BlockSpec((bm, bn), lambda i, j: (i, j)),
        out_shape=jax.ShapeDtypeStruct((m, n), F32),
        compiler_params=_cparams(("parallel", "parallel")),
        name=name,
    )(a1, a2, w, w, res)


def _mm_res_acc_kernel(a_ref, w_ref, r_ref, o_ref, acc_ref):
    k = pl.program_id(2)

    @pl.when(k == 0)
    def _():
        acc_ref[...] = jnp.zeros(acc_ref.shape, F32)

    acc_ref[...] += _dot(a_ref[...], w_ref[...])

    @pl.when(k == pl.num_programs(2) - 1)
    def _():
        o_ref[...] = r_ref[...] + acc_ref[...]


def _mm_res_acc(a, w, res, bk, bm=1024, bn=1024, name="mm_res_acc"):
    m, k = a.shape
    n = w.shape[1]
    return pl.pallas_call(
        _mm_res_acc_kernel,
        grid=(m // bm, n // bn, k // bk),
        in_specs=[pl.BlockSpec((bm, bk), lambda i, j, kk: (i, kk)),
                  pl.BlockSpec((bk, bn), lambda i, j, kk: (kk, j)),
                  pl.BlockSpec((bm, bn), lambda i, j, kk: (i, j))],
        out_specs=pl.BlockSpec((bm, bn), lambda i, j, kk: (i, j)),
        out_shape=jax.ShapeDtypeStruct((m, n), F32),
        scratch_shapes=[pltpu.VMEM((bm, bn), F32)],
        compiler_params=_cparams(("parallel", "parallel", "arbitrary")),
        name=name,
    )(a, w, res)


def _rope_tables(seq, rot, width):
    half = rot // 2
    inv_freq = 1.0 / (ROPE_THETA ** (jnp.arange(half, dtype=F32) * (2.0 / rot)))
    ang = jnp.arange(seq, dtype=jnp.int32).astype(F32)[:, None] * inv_freq[None, :]
    cos, sin = jnp.cos(ang), jnp.sin(ang)
    zh = jnp.zeros((seq, half), F32)
    rest = width - rot
    c = jnp.concatenate([cos, cos, jnp.ones((seq, rest), F32)], axis=1)
    sa = jnp.concatenate([zh, sin, jnp.zeros((seq, rest), F32)], axis=1)
    sb = jnp.concatenate([-sin, zh, jnp.zeros((seq, rest), F32)], axis=1)
    return c, sa, sb


def _apply_rope(y, c, sa, sb, half):
    width = y.shape[-1]
    return y * c + pltpu.roll(y, half, 1) * sa + pltpu.roll(y, width - half, 1) * sb


def _lat_kernel(a_ref, w_ref, gq_ref, gkv_ref, cq_ref, ckv_ref, kpe_ref):
    z = _dot(a_ref[...], w_ref[...])
    c1 = MLA_Q_RANK
    c2 = c1 + MLA_KV_RANK
    cq = z[:, :c1]
    ckv = z[:, c1:c2]
    cq = cq * lax.rsqrt(jnp.mean(cq * cq, axis=-1, keepdims=True) + NORM_EPS)
    ckv = ckv * lax.rsqrt(jnp.mean(ckv * ckv, axis=-1, keepdims=True) + NORM_EPS)
    cq_ref[...] = (cq * gq_ref[...]).astype(cq_ref.dtype)
    ckv_ref[...] = (ckv * gkv_ref[...]).astype(ckv_ref.dtype)
    kpe_ref[...] = z[:, c2:]


def _lat_proj(h, w_lat, gq, gkv, bm=512):
    m, k = h.shape
    return pl.pallas_call(
        _lat_kernel,
        grid=(m // bm,),
        in_specs=[pl.BlockSpec((bm, k), lambda i: (i, 0)),
                  pl.BlockSpec((k, LAT_PAD), lambda i: (0, 0)),
                  pl.BlockSpec((1, MLA_Q_RANK), lambda i: (0, 0)),
                  pl.BlockSpec((1, MLA_KV_RANK), lambda i: (0, 0))],
        out_specs=[pl.BlockSpec((bm, MLA_Q_RANK), lambda i: (i, 0)),
                   pl.BlockSpec((bm, MLA_KV_RANK), lambda i: (i, 0)),
                   pl.BlockSpec((bm, 128), lambda i: (i, 0))],
        out_shape=[jax.ShapeDtypeStruct((m, MLA_Q_RANK), BF16),
                   jax.ShapeDtypeStruct((m, MLA_KV_RANK), BF16),
                   jax.ShapeDtypeStruct((m, 128), F32)],
        compiler_params=_cparams(("parallel",)),
        name="lat_proj",
    )(h, w_lat, gq.reshape(1, -1), gkv.reshape(1, -1))


def _mla_q_kernel(a_ref, w_ref, g_ref, c_ref, sa_ref, sb_ref, o_ref, *, heads):
    z = _dot(a_ref[...], w_ref[...])
    g = g_ref[...]
    c, sa, sb = c_ref[...], sa_ref[...], sb_ref[...]
    for h in range(heads):
        zh = z[:, h * MLA_HEAD_PAD:(h + 1) * MLA_HEAD_PAD]
        ms = jnp.sum(zh * zh, axis=-1, keepdims=True) * (1.0 / MLA_QK)
        y = zh * lax.rsqrt(ms + NORM_EPS) * g
        o_ref[:, h * MLA_HEAD_PAD:h * MLA_HEAD_PAD + 128] = y[:, :128].astype(o_ref.dtype)
        yr = _apply_rope(y[:, 128:], c, sa, sb, MLA_ROPE // 2)
        o_ref[:, h * MLA_HEAD_PAD + 128:(h + 1) * MLA_HEAD_PAD] = yr.astype(o_ref.dtype)


def _mla_q_proj(cq, w_uq, g_pad, tabs, seq, bm=1024, heads=4):
    m, k = cq.shape
    n = w_uq.shape[1]
    bn = heads * MLA_HEAD_PAD
    nsb = seq // bm
    tab_spec = pl.BlockSpec((bm, 128), lambda i, j: (i % nsb, 0))
    return pl.pallas_call(
        functools.partial(_mla_q_kernel, heads=heads),
        grid=(m // bm, n // bn),
        in_specs=[pl.BlockSpec((bm, k), lambda i, j: (i, 0)),
                  pl.BlockSpec((k, bn), lambda i, j: (0, j)),
                  pl.BlockSpec((1, MLA_HEAD_PAD), lambda i, j: (0, 0)),
                  tab_spec, tab_spec, tab_spec],
        out_specs=pl.BlockSpec((bm, bn), lambda i, j: (i, j)),
        out_shape=jax.ShapeDtypeStruct((m, n), BF16),
        compiler_params=_cparams(("parallel", "parallel")),
        name="mla_q_proj",
    )(cq, w_uq, g_pad, *tabs)


def _mla_kv_kernel(a_ref, w_ref, kpe_ref, g_ref, c_ref, sa_ref, sb_ref, k_ref, v_ref, *, heads):
    z = _dot(a_ref[...], w_ref[...])
    kpe = kpe_ref[...]
    g = g_ref[...]
    g_nope, g_rope = g[:, :128], g[:, 128:]
    c, sa, sb = c_ref[...], sa_ref[...], sb_ref[...]
    ss_pe = jnp.sum(kpe * kpe, axis=-1, keepdims=True)
    one_col = (lax.broadcasted_iota(jnp.int32, (1, 128), 1) == 0).astype(v_ref.dtype)
    for h in range(heads):
        nope = z[:, h * 256:h * 256 + 128]
        ms = (jnp.sum(nope * nope, axis=-1, keepdims=True) + ss_pe) * (1.0 / MLA_QK)
        r = lax.rsqrt(ms + NORM_EPS)
        k_ref[:, h * MLA_HEAD_PAD:h * MLA_HEAD_PAD + 128] = (nope * r * g_nope).astype(k_ref.dtype)
        kr = _apply_rope(kpe * r * g_rope, c, sa, sb, MLA_ROPE // 2)
        k_ref[:, h * MLA_HEAD_PAD + 128:(h + 1) * MLA_HEAD_PAD] = kr.astype(k_ref.dtype)
        v_ref[:, h * MLA_V_PAD:h * MLA_V_PAD + MLA_V] = z[:, h * 256 + 128:(h + 1) * 256].astype(v_ref.dtype)
        v_ref[:, h * MLA_V_PAD + MLA_V:(h + 1) * MLA_V_PAD] = jnp.broadcast_to(one_col, (z.shape[0], 128))


def _mla_kv_proj(ckv, w_ukv, kpe, g_pad, tabs, seq, bm=1024, heads=4):
    m, k = ckv.shape
    nsb = seq // bm
    tab_spec = pl.BlockSpec((bm, 128), lambda i, j: (i % nsb, 0))
    return pl.pallas_call(
        functools.partial(_mla_kv_kernel, heads=heads),
        grid=(m // bm, MLA_HEADS // heads),
        in_specs=[pl.BlockSpec((bm, k), lambda i, j: (i, 0)),
                  pl.BlockSpec((k, heads * 256), lambda i, j: (0, j)),
                  pl.BlockSpec((bm, 128), lambda i, j: (i, 0)),
                  pl.BlockSpec((1, MLA_HEAD_PAD), lambda i, j: (0, 0)),
                  tab_spec, tab_spec, tab_spec],
        out_specs=[pl.BlockSpec((bm, heads * MLA_HEAD_PAD), lambda i, j: (i, j)),
                   pl.BlockSpec((bm, heads * MLA_V_PAD), lambda i, j: (i, j))],
        out_shape=[jax.ShapeDtypeStruct((m, MLA_HEADS * MLA_HEAD_PAD), BF16),
                   jax.ShapeDtypeStruct((m, MLA_HEADS * MLA_V_PAD), BF16)],
        compiler_params=_cparams(("parallel", "parallel")),
        name="mla_kv_proj",
    )(ckv, w_ukv, kpe, g_pad, *tabs)


def _heads_kernel(a_ref, w_ref, g_ref, o_ref, *, heads, n_norm):
    z = _dot(a_ref[...], w_ref[...])
    j = pl.program_id(1)

    @pl.when(j < n_norm)
    def _():
        g = g_ref[0]
        for h in range(heads):
            zh = z[:, h * 128:(h + 1) * 128]
            y = zh * lax.rsqrt(jnp.mean(zh * zh, axis=-1, keepdims=True) + NORM_EPS) * g
            o_ref[:, h * 128:(h + 1) * 128] = y.astype(o_ref.dtype)

    @pl.when(j >= n_norm)
    def _():
        o_ref[...] = z.astype(o_ref.dtype)


def _heads_proj(a, w, gains, n_q_blocks, bm=1024, bn=1024, name="heads_proj"):
    m, k = a.shape
    n = w.shape[1]
    g3 = jnp.concatenate([gains, jnp.ones((1, 128), F32)], axis=0).reshape(3, 1, 128)
    return pl.pallas_call(
        functools.partial(_heads_kernel, heads=bn // 128, n_norm=2 * n_q_blocks),
        grid=(m // bm, n // bn),
        in_specs=[pl.BlockSpec((bm, k), lambda i, j: (i, 0)),
                  pl.BlockSpec((k, bn), lambda i, j: (0, j)),
                  pl.BlockSpec((1, 1, 128), lambda i, j: (j // n_q_blocks, 0, 0))],
        out_specs=pl.BlockSpec((bm, bn), lambda i, j: (i, j)),
        out_shape=jax.ShapeDtypeStruct((m, n), BF16),
        compiler_params=_cparams(("parallel", "parallel")),
        name=name,
    )(a, w, g3)


def _flash_stages(q_ref, k_ref, v_ref, o_ref, s_w, s_r, p_w, p_r, al_w, al_r, mr_w, mr_r, m_ref,
                  acc_ref, *, total, nk):
    t = pl.program_id(0)
    s = _dot_nt(q_ref[...], k_ref[...])
    s_w[...] = s
    mr_w[...] = jnp.max(s, axis=-1, keepdims=True)

    valid_b = jnp.logical_and(t >= 1, t <= total)
    ki_b = jnp.clip(t - 1, 0, total - 1) % nk
    m_old = m_ref[...]
    m_prev = jnp.where(ki_b == 0, -jnp.inf, m_old)
    m_new = jnp.maximum(m_prev, mr_r[...])
    al_w[...] = jnp.exp2(m_prev - m_new)
    p_w[...] = jnp.exp2(s_r[...] - m_new).astype(p_w.dtype)
    m_ref[...] = jnp.where(valid_b, m_new, m_old)

    acc = al_r[...] * acc_ref[...] + _dot(p_r[...], v_ref[...])
    acc_ref[...] = acc
    ki_c = jnp.clip(t - 2, 0, total - 1) % nk

    @pl.when(jnp.logical_and(t >= 2, ki_c == nk - 1))
    def _():
        o_ref[...] = (acc[:, :MLA_V] / acc[:, MLA_V:MLA_V + 1]).astype(o_ref.dtype)


def _flash_kernel(q_ref, k_ref, v_ref, o_ref, s0, s1, p0, p1, al0, al1, mr0, mr1, m_ref, acc_ref, **kw):
    t = pl.program_id(0)
    s_scr, p_scr, al_scr, mr_scr = (s0, s1), (p0, p1), (al0, al1), (mr0, mr1)

    @pl.when(t == 0)
    def _():
        for ref in (s0, s1, p0, p1, al0, al1, mr0, mr1, m_ref, acc_ref):
            ref[...] = jnp.zeros(ref.shape, ref.dtype)

    for par in (0, 1):
        @pl.when(t % 2 == par)
        def _(par=par):
            _flash_stages(q_ref, k_ref, v_ref, o_ref, s_scr[par], s_scr[1 - par],
                          p_scr[1 - par], p_scr[par], al_scr[1 - par], al_scr[par],
                          mr_scr[par], mr_scr[1 - par], m_ref, acc_ref, **kw)


def _mla_attention(q, k, v, batch, seq, tq=1024, tk=2048):
    m = q.shape[0]
    tk = min(tk, seq // 2)
    nq, nk = seq // tq, seq // tk
    total = batch * MLA_HEADS * nq * nk

    def unravel(tt):
        ki = tt % nk
        r = tt // nk
        qi = r % nq
        r = r // nq
        return r // MLA_HEADS, r % MLA_HEADS, qi, ki

    def q_map(t):
        b, h, qi, _ = unravel(jnp.minimum(t, total - 1))
        return (b * nq + qi, h)

    def k_map(t):
        b, h, _, ki = unravel(jnp.minimum(t, total - 1))
        return (b * nk + ki, h)

    def v_map(t):
        b, h, _, ki = unravel(jnp.clip(t - 2, 0, total - 1))
        return (b * nk + ki, h)

    def o_map(t):
        b, h, qi, _ = unravel(jnp.clip(t - 2, 0, total - 1))
        return (b * nq + qi, h)

    def col():
        return pltpu.VMEM((tq, 1), F32)

    return pl.pallas_call(
        functools.partial(_flash_kernel, total=total, nk=nk),
        grid=(total + 2,),
        in_specs=[pl.BlockSpec((tq, MLA_HEAD_PAD), q_map),
                  pl.BlockSpec((tk, MLA_HEAD_PAD), k_map),
                  pl.BlockSpec((tk, MLA_V_PAD), v_map)],
        out_specs=pl.BlockSpec((tq, MLA_V), o_map),
        out_shape=jax.ShapeDtypeStruct((m, MLA_HEADS * MLA_V), BF16),
        scratch_shapes=[pltpu.VMEM((tq, tk), F32), pltpu.VMEM((tq, tk), F32),
                        pltpu.VMEM((tq, tk), BF16), pltpu.VMEM((tq, tk), BF16),
                        col(), col(), col(), col(), col(), pltpu.VMEM((tq, MLA_V_PAD), F32)],
        compiler_params=_cparams(("arbitrary",)),
        name="mla_attention",
    )(q, k, v)


DIL_SQ = 256
DIL_SK = DIL_SQ + 2 * DIL_RADIUS


def _dil_kernel(q_ref, kp_ref, k_ref, kn_ref, vp_ref, v_ref, vn_ref, o_ref, st_ref,
                kwin_ref, vwin_ref, *, tq, length):
    r = DIL_RADIUS
    q0 = pl.program_id(2) * tq
    kwin_ref[0:r] = kp_ref[...]
    kwin_ref[r:r + tq] = k_ref[...]
    kwin_ref[r + tq:] = kn_ref[...]
    vwin_ref[0:r] = vp_ref[...]
    vwin_ref[r:r + tq] = v_ref[...]
    vwin_ref[r + tq:] = vn_ref[...]
    qi = lax.broadcasted_iota(jnp.int32, (DIL_SQ, DIL_SK), 0)
    kj = lax.broadcasted_iota(jnp.int32, (DIL_SQ, DIL_SK), 1)
    band = (kj >= qi) & (kj <= qi + 2 * r)
    lane = lax.broadcasted_iota(jnp.int32, (1, 128), 1)
    for c in range(tq // DIL_SQ):
        kpos = q0 + c * DIL_SQ - r + kj
        valid = band & (kpos >= 0) & (kpos < length)
        st = jnp.zeros((DIL_SQ, 128), F32)
        for h in range(DIL_HPG):
            hs = slice(h * 128, (h + 1) * 128)
            qh = q_ref[c * DIL_SQ:(c + 1) * DIL_SQ, hs]
            kh = kwin_ref[c * DIL_SQ:c * DIL_SQ + DIL_SK, hs]
            vh = vwin_ref[c * DIL_SQ:c * DIL_SQ + DIL_SK, hs]
            s = jnp.where(valid, _dot_nt(qh, kh), NEG_INF)
            mx = jnp.max(s, axis=-1, keepdims=True)
            p = jnp.exp2(s - mx)
            den = jnp.sum(p, axis=-1, keepdims=True)
            o = _dot(p.astype(vh.dtype), vh) / den
            o_ref[c * DIL_SQ:(c + 1) * DIL_SQ, hs] = o
            st = jnp.where(lane == h, mx, st)
            st = jnp.where(lane == DIL_HPG + h, den, st)
        st_ref[c * DIL_SQ:(c + 1) * DIL_SQ, :] = st


def _dil_proj_kernel(a_ref, w_ref, g_ref, c_ref, sa_ref, sb_ref, o_ref, y_scr, *, dil, heads):
    z = _dot(a_ref[...], w_ref[...])
    part = pl.program_id(1)
    rows = z.shape[0] // dil

    @pl.when(part < 2)
    def _():
        g = g_ref[0]
        for h in range(heads):
            zh = z[:, h * 128:(h + 1) * 128]
            y = zh * lax.rsqrt(jnp.mean(zh * zh, axis=-1, keepdims=True) + NORM_EPS) * g
            y_scr[h] = _apply_rope(y, c_ref[...], sa_ref[...], sb_ref[...], DIL_ROT // 2)

    @pl.when(part == 2)
    def _():
        for h in range(heads):
            y_scr[h] = z[:, h * 128:(h + 1) * 128]

    for h in range(heads):
        hs = slice(h * 128, (h + 1) * 128)
        if dil == 1:
            o_ref[0, :, hs] = y_scr[h].astype(o_ref.dtype)
        else:
            for r in range(dil):
                o_ref[r, :, hs] = y_scr[h, pl.ds(r, rows, stride=dil), :].astype(o_ref.dtype)


def _dil_proj(h, w_b, gains, tabs, batch, seq, group, dil, bm=1024):
    m, k = h.shape
    w = DIL_HPG * 128
    nsb = seq // bm
    g3 = jnp.concatenate([gains, jnp.ones((1, 128), F32)], axis=0).reshape(3, 1, 128)
    tab_spec = pl.BlockSpec((bm, 128), lambda i, p: (i % nsb, 0))
    return pl.pallas_call(
        functools.partial(_dil_proj_kernel, dil=dil, heads=DIL_HPG),
        grid=(m // bm, 3),
        in_specs=[pl.BlockSpec((bm, k), lambda i, p: (i, 0)),
                  pl.BlockSpec((k, w), lambda i, p: (0, p * 3 + group)),
                  pl.BlockSpec((1, 1, 128), lambda i, p: (p, 0, 0)),
                  tab_spec, tab_spec, tab_spec],
        out_specs=pl.BlockSpec((None, dil, bm // dil, w), lambda i, p: (i // nsb, 0, i % nsb, p)),
        out_shape=jax.ShapeDtypeStruct((batch, dil, seq // dil, 3 * w), BF16),
        scratch_shapes=[pltpu.VMEM((DIL_HPG, bm, 128), F32)],
        compiler_params=_cparams(("parallel", "arbitrary")),
        name=f"dil_proj_g{group}",
    )(h, w_b, g3, *tabs)


def _dilated_group(qkv, group):
    batch, dil, length, _ = qkv.shape
    tq = min(512, length)
    r = DIL_RADIUS
    nq = length // tq
    hb = tq // r
    nhb = length // r
    w = DIL_HPG * 128

    def main(part):
        return pl.BlockSpec((None, None, tq, w), lambda b, rr, qi: (b, rr, qi, part))

    def prev(part):
        return pl.BlockSpec((None, None, r, w),
                            lambda b, rr, qi: (b, rr, jnp.maximum(qi * hb - 1, 0), part))

    def nxt(part):
        return pl.BlockSpec((None, None, r, w),
                            lambda b, rr, qi: (b, rr, jnp.minimum((qi + 1) * hb, nhb - 1), part))

    return pl.pallas_call(
        functools.partial(_dil_kernel, tq=tq, length=length),
        grid=(batch, dil, nq),
        in_specs=[main(0), prev(1), main(1), nxt(1), prev(2), main(2), nxt(2)],
        out_specs=[pl.BlockSpec((None, None, tq, w), lambda b, rr, qi: (b, rr, qi, 0)),
                   pl.BlockSpec((None, None, tq, 128), lambda b, rr, qi: (b, rr, qi, 0))],
        out_shape=[jax.ShapeDtypeStruct((batch, dil, length, w), F32),
                   jax.ShapeDtypeStruct((batch, dil, length, 128), F32)],
        scratch_shapes=[pltpu.VMEM((tq + 2 * r, w), BF16), pltpu.VMEM((tq + 2 * r, w), BF16)],
        compiler_params=_cparams(("parallel", "parallel", "parallel")),
        name=f"dilated_g{group}",
    )(qkv, qkv, qkv, qkv, qkv, qkv, qkv)


def _dil_merge_kernel(o0_ref, o1_ref, o2_ref, s0_ref, s1_ref, s2_ref, out_ref, on_scr, sn_scr,
                      *, dils):
    bm = out_ref.shape[0]
    for gi, (o_ref, s_ref) in enumerate(((o0_ref, s0_ref), (o1_ref, s1_ref), (o2_ref, s2_ref))):
        dil = dils[gi]
        for r in range(dil):
            rows = pl.ds(r, bm // dil, stride=dil) if dil > 1 else slice(None)
            sn_scr[gi, rows, :] = s_ref[r]
            for h in range(DIL_HPG):
                on_scr[gi, h, rows, :] = o_ref[r, :, h * 128:(h + 1) * 128]
    s0, s1, s2 = sn_scr[0], sn_scr[1], sn_scr[2]
    for h in range(DIL_HPG):
        hs = slice(h * 128, (h + 1) * 128)
        m0, m1, m2 = s0[:, h:h + 1], s1[:, h:h + 1], s2[:, h:h + 1]
        l0, l1, l2 = (s0[:, DIL_HPG + h:DIL_HPG + h + 1], s1[:, DIL_HPG + h:DIL_HPG + h + 1],
                      s2[:, DIL_HPG + h:DIL_HPG + h + 1])
        mx = jnp.maximum(jnp.maximum(m0, m1), m2)
        w0 = jnp.exp2(m0 - mx) * l0
        w1 = jnp.exp2(m1 - mx) * l1
        w2 = jnp.exp2(m2 - mx) * l2
        num = w0 * on_scr[0, h] + w1 * on_scr[1, h] + w2 * on_scr[2, h]
        out_ref[:, hs] = (num / (w0 + w1 + w2)).astype(out_ref.dtype)


def _dil_merge(outs, stats, batch, seq, bm=512):
    w = DIL_HPG * 128
    nsb = seq // bm
    dils = tuple(o.shape[1] for o in outs)

    def spec(dil, width):
        return pl.BlockSpec((None, dil, bm // dil, width), lambda b, i: (b, 0, i, 0))

    return pl.pallas_call(
        functools.partial(_dil_merge_kernel, dils=dils),
        grid=(batch, nsb),
        in_specs=[spec(d, w) for d in dils] + [spec(d, 128) for d in dils],
        out_specs=pl.BlockSpec((bm, w), lambda b, i: (b * nsb + i, 0)),
        out_shape=jax.ShapeDtypeStruct((batch * seq, w), BF16),
        scratch_shapes=[pltpu.VMEM((3, DIL_HPG, bm, 128), F32), pltpu.VMEM((3, bm, 128), F32)],
        compiler_params=_cparams(("parallel", "parallel")),
        name="dilated_merge",
    )(*outs, *stats)


NA_TQ = NA_QROWS * GRID_W
NA_TK = NA_KROWS * GRID_W
NA_HPS = 8


def _na_bias_tables(rpb, rows):
    heads = rpb.shape[0]
    cols = np.arange(GRID_W)
    cs = np.clip(cols - NA_COLS // 2, 0, GRID_W - NA_COLS)
    kc = np.arange(GRID_W)
    col_ok = (kc[None, :] >= cs[:, None]) & (kc[None, :] < cs[:, None] + NA_COLS)
    col_idx = np.clip(kc[None, :] - cols[:, None] + (NA_COLS - 1), 0, 2 * NA_COLS - 2)
    toe = jnp.where(col_ok[None, None], (rpb * LOG2E)[:, :, col_idx], NEG_INF)
    toe = jnp.pad(toe.transpose(0, 2, 1, 3), ((0, 0), (0, 0), (NA_KROWS, NA_KROWS), (0, 0)),
                  constant_values=NEG_INF)
    tabs = []
    for i0, ks in ((0, 0), (NA_QROWS * 2, NA_QROWS), (rows - NA_QROWS, rows - NA_KROWS)):
        qi = i0 + np.arange(NA_QROWS)
        rs = np.clip(qi - NA_ROWS // 2, 0, rows - NA_ROWS)
        kr = ks + np.arange(NA_KROWS)
        row_ok = (kr[None, :] >= rs[:, None]) & (kr[None, :] < rs[:, None] + NA_ROWS)
        lo = ks - qi + (NA_ROWS - 1) + NA_KROWS
        t = jnp.stack([toe[:, :, int(l):int(l) + NA_KROWS] for l in lo], axis=1)
        t = jnp.where(row_ok[None, :, None, :, None], t, NEG_INF)
        tabs.append(t.reshape(heads, NA_TQ, NA_TK))
    return jnp.stack(tabs)


def _na_kernel(q_ref, k0_ref, k1_ref, k2_ref, v0_ref, v1_ref, v2_ref, b_ref, o_ref):
    for h in range(NA_HPS):
        hs = slice(h * 128, (h + 1) * 128)
        q = q_ref[0, :, hs]
        k = jnp.concatenate([k0_ref[0, :, hs], k1_ref[0, :, hs], k2_ref[0, :, hs]], axis=0)
        v = jnp.concatenate([v0_ref[0, :, hs], v1_ref[0, :, hs], v2_ref[0, :, hs]], axis=0)
        s = _dot_nt(q, k) + b_ref[0, h]
        mx = jnp.max(s, axis=-1, keepdims=True)
        p = jnp.exp2(s - mx)
        den = jnp.sum(p, axis=-1, keepdims=True)
        o = _dot(p.astype(v.dtype), v) / den
        o_ref[0, :, hs] = o.astype(o_ref.dtype)


def _na_attention(qkv, bias, batch, seq):
    rows = seq // GRID_W
    nblk = rows // NA_QROWS
    nkb = NA_KROWS // NA_QROWS
    ng = NA_HEADS // NA_HPS
    w = NA_HPS * 128
    qv = qkv.reshape(batch, seq, qkv.shape[1])

    def kv_spec(part, off):
        return pl.BlockSpec(
            (1, NA_TQ, w),
            lambda b, g, rb: (b, jnp.clip(rb - 1, 0, nblk - nkb) + off, part * ng + g))

    def bias_map(b, g, rb):
        cfg = jnp.where(rb == 0, 0, jnp.where(rb == nblk - 1, 2, 1))
        return (cfg, g, 0, 0)

    o = pl.pallas_call(
        _na_kernel,
        grid=(batch, ng, nblk),
        in_specs=[pl.BlockSpec((1, NA_TQ, w), lambda b, g, rb: (b, rb, g)),
                  kv_spec(1, 0), kv_spec(1, 1), kv_spec(1, 2),
                  kv_spec(2, 0), kv_spec(2, 1), kv_spec(2, 2),
                  pl.BlockSpec((1, NA_HPS, NA_TQ, NA_TK), bias_map)],
        out_specs=pl.BlockSpec((1, NA_TQ, w), lambda b, g, rb: (b, rb, g)),
        out_shape=jax.ShapeDtypeStruct((batch, seq, NA_HEADS * 128), BF16),
        compiler_params=_cparams(("parallel", "parallel", "arbitrary")),
        name="na_attention",
    )(qv, qv, qv, qv, qv, qv, qv, bias)
    return o.reshape(batch * seq, NA_HEADS * 128)


def _ffn_up_kernel(h_ref, w_ref, a_ref, ap_ref, an_ref, cw_ref, cb_ref, o_ref, *, bm, seq):
    i = pl.program_id(0)
    u = _dot(h_ref[...], w_ref[...])
    a = a_ref[...]
    first = (i * bm) % seq == 0
    last = ((i + 1) * bm) % seq == 0
    prev_row = jnp.where(first, 0.0, ap_ref[7:8, :])
    next_row = jnp.where(last, 0.0, an_ref[0:1, :])
    ridx = lax.broadcasted_iota(jnp.int32, (bm, 1), 0)
    a_prev = jnp.where(ridx == 0, prev_row, pltpu.roll(a, 1, 0))
    a_next = jnp.where(ridx == bm - 1, next_row, pltpu.roll(a, bm - 1, 0))
    cw = cw_ref[...]
    ac = a_prev * cw[0:1] + a * cw[1:2] + a_next * cw[2:3] + cb_ref[...]
    o_ref[...] = (ac * jax.nn.sigmoid(ac) * u).astype(o_ref.dtype)


def _ffn_up(h, w_up, a, conv_w, conv_b, seq, bm=1024, bn=512):
    m, k = h.shape
    n = w_up.shape[1]
    rb = bm // 8
    nrb = m // 8
    return pl.pallas_call(
        functools.partial(_ffn_up_kernel, bm=bm, seq=seq),
        grid=(m // bm, n // bn),
        in_specs=[pl.BlockSpec((bm, k), lambda i, j: (i, 0)),
                  pl.BlockSpec((k, bn), lambda i, j: (0, j)),
                  pl.BlockSpec((bm, bn), lambda i, j: (i, j)),
                  pl.BlockSpec((8, bn), lambda i, j: (jnp.maximum(i * rb - 1, 0), j)),
                  pl.BlockSpec((8, bn), lambda i, j: (jnp.minimum((i + 1) * rb, nrb - 1), j)),
                  pl.BlockSpec((3, bn), lambda i, j: (0, j)),
                  pl.BlockSpec((1, bn), lambda i, j: (0, j))],
        out_specs=pl.BlockSpec((bm, bn), lambda i, j: (i, j)),
        out_shape=jax.ShapeDtypeStruct((m, n), BF16),
        compiler_params=_cparams(("parallel", "parallel")),
        name="ffn_up",
    )(h, w_up, a, a, a, conv_w, conv_b)


def _conv_ffn(x2d, seq, norm_g, w_gate, w_up, conv_w, conv_b, w_down, layer):
    pad = D_FF_PAD - D_FF
    wg = _cast_pad(w_gate, layer, cols_out=D_FF_PAD)
    wu = _cast_pad(w_up, layer, cols_out=D_FF_PAD)
    wd = _cast_pad(w_down, layer, rows_out=D_FF_PAD)
    cw = jnp.pad(conv_w, ((0, 0), (0, pad)))
    cb = jnp.pad(conv_b, ((0, pad),)).reshape(1, D_FF_PAD)
    h = _rmsnorm(x2d, norm_g)
    a = _mm(h, wg, F32, name="ffn_gate")
    g = _ffn_up(h, wu, a, cw, cb, seq)
    return _mm_res_acc(g, wd, x2d, bk=D_FF_PAD // 4, name="ffn_down")


def _even_layer(x2d, batch, seq, e_norm, w_in, q_lora_g, kv_lora_g, w_uq, w_ukv, mla_qn, mla_kn,
                dil_qn, dil_kn, w_out, layer):
    c3 = MLA_Q_RANK + MLA_KV_RANK + MLA_ROPE
    w_lat = jnp.pad(w_in[:, :c3], ((0, 0), (0, LAT_PAD - c3))).astype(BF16)
    w_b = w_in[:, c3:].astype(BF16)
    hpad = MLA_HEAD_PAD - MLA_QK
    w_uq_p = jnp.pad(w_uq.reshape(MLA_Q_RANK, MLA_HEADS, MLA_QK), ((0, 0), (0, 0), (0, hpad)))
    w_uq_p = w_uq_p.reshape(MLA_Q_RANK, MLA_HEADS * MLA_HEAD_PAD).astype(BF16)
    gq_pad = jnp.pad(mla_qn * (MLA_QK ** -0.5 * LOG2E), (0, hpad)).reshape(1, MLA_HEAD_PAD)
    gk_pad = jnp.pad(mla_kn, (0, hpad)).reshape(1, MLA_HEAD_PAD)
    mla_tabs = _rope_tables(seq, MLA_ROPE, 128)
    dil_tabs = _rope_tables(seq, DIL_ROT, 128)

    h = _rmsnorm(x2d, e_norm)
    cq, ckv, kpe = _lat_proj(h, w_lat, q_lora_g, kv_lora_g)
    q = _mla_q_proj(cq, w_uq_p, gq_pad, mla_tabs, seq)
    k, v = _mla_kv_proj(ckv, w_ukv.astype(BF16), kpe, gk_pad, mla_tabs, seq)
    o_a = _mla_attention(q, k, v, batch, seq)

    outs, stats = [], []
    for g, (window, dil) in enumerate(DIL_CONFIGS):
        assert window // (2 * dil) == DIL_RADIUS
        qkv_g = _dil_proj(h, w_b, jnp.stack([dil_qn * (HEAD_DIM ** -0.5 * LOG2E), dil_kn]), dil_tabs,
                          batch, seq, g, dil)
        o, st = _dilated_group(qkv_g, g)
        outs.append(o)
        stats.append(st)
    o_b = _dil_merge(outs, stats, batch, seq)
    return _mm_res2(o_a, o_b, _cast_pad(w_out, layer), x2d, name="even_out_proj")


def _odd_layer(x2d, batch, seq, o_norm, w_qkv, qn, kn, rpb, w_out, layer):
    h = _rmsnorm(x2d, o_norm)
    qkv = _heads_proj(h, _cast_pad(w_qkv, layer),
                      jnp.stack([qn * (HEAD_DIM ** -0.5 * LOG2E), kn]), NA_HEADS * 128 // 1024,
                      name="na_qkv_proj")
    bias = _na_bias_tables(rpb, seq // GRID_W)
    o = _na_attention(qkv, bias, batch, seq)
    return _mm_res(o, _cast_pad(w_out, layer), x2d, name="odd_out_proj")


def kernel(x, e_norm, e_w_in, e_q_lora_norm, e_kv_lora_norm, e_w_uq, e_w_ukv, e_mla_q_norm,
           e_mla_k_norm, e_dil_q_norm, e_dil_k_norm, e_w_out, o_norm, o_w_qkv, o_q_norm, o_k_norm,
           o_rpb, o_w_out, f_norm, f_w_gate, f_w_up, f_conv_w, f_conv_b, f_w_down):
    batch, seq, d = x.shape
    x2d = x.reshape(batch * seq, d)
    depth = f_norm.shape[0]
    for layer in range(depth):
        i = layer // 2
        if layer % 2 == 0:
            x2d = _even_layer(x2d, batch, seq, e_norm[i], e_w_in[i], e_q_lora_norm[i],
                              e_kv_lora_norm[i], e_w_uq[i], e_w_ukv[i], e_mla_q_norm[i],
                              e_mla_k_norm[i], e_dil_q_norm[i], e_dil_k_norm[i], e_w_out, i)
        else:
            x2d = _odd_layer(x2d, batch, seq, o_norm[i], o_w_qkv, o_q_norm[i], o_k_norm[i],
                             o_rpb[i], o_w_out, i)
        x2d = _conv_ffn(x2d, seq, f_norm[layer], f_w_gate, f_w_up, f_conv_w[layer],
                        f_conv_b[layer], f_w_down, layer)
    return x2d.reshape(batch, seq, d)
```

```python
import functools

import jax
import jax.numpy as jnp
import numpy as np
from jax import lax
from jax.experimental import pallas as pl
from jax.experimental.pallas import tpu as pltpu

F32 = jnp.float32
BF16 = jnp.bfloat16

HEAD_DIM = 128
ROPE_THETA = 500000.0
NORM_EPS = 1e-6
NEG_INF = -1e30

MLA_HEADS = 16
MLA_Q_RANK = 896
MLA_KV_RANK = 512
MLA_NOPE = 128
MLA_ROPE = 64
MLA_QK = MLA_NOPE + MLA_ROPE
MLA_V = 128
MLA_HEAD_PAD = 256
MLA_V_PAD = 256
LOG2E = float(np.log2(np.e))
LAT_PAD = 1536

DIL_CONFIGS = ((128, 1), (512, 4), (2048, 16))
DIL_HPG = 8
DIL_HEADS = 24
DIL_ROT = HEAD_DIM // 4
DIL_RADIUS = 64

GRID_W = 64
NA_HEADS = 32
NA_ROWS = 8
NA_COLS = 16
NA_QROWS = 4
NA_KROWS = 12

D_FF = 11008
D_FF_PAD = 11264

VMEM_LIMIT_BYTES = 56 * 1024 * 1024


def _cparams(sem):
    return pltpu.CompilerParams(dimension_semantics=sem, vmem_limit_bytes=VMEM_LIMIT_BYTES)


def _dot(a, b):
    return jnp.dot(a, b, preferred_element_type=F32)


def _dot_nt(a, b):
    return lax.dot_general(a, b, (((1,), (1,)), ((), ())), preferred_element_type=F32)


def _rmsnorm_kernel(x_ref, g_ref, o_ref):
    x = x_ref[...]
    y = x * lax.rsqrt(jnp.mean(x * x, axis=-1, keepdims=True) + NORM_EPS)
    o_ref[...] = (y * g_ref[...]).astype(o_ref.dtype)


def _rmsnorm(x2d, g, bm=512):
    m, d = x2d.shape
    return pl.pallas_call(
        _rmsnorm_kernel,
        grid=(m // bm,),
        in_specs=[pl.BlockSpec((bm, d), lambda i: (i, 0)),
                  pl.BlockSpec((1, d), lambda i: (0, 0))],
        out_specs=pl.BlockSpec((bm, d), lambda i: (i, 0)),
        out_shape=jax.ShapeDtypeStruct((m, d), BF16),
        compiler_params=_cparams(("parallel",)),
        name="rmsnorm",
    )(x2d, g.reshape(1, d))


def _cast_kernel(w_ref, o_ref, *, n_in_blocks, cols):
    i = pl.program_id(0)

    @pl.when(i < n_in_blocks)
    def _():
        o_ref[:, :cols] = w_ref[...].astype(o_ref.dtype)
        if o_ref.shape[1] > cols:
            o_ref[:, cols:] = jnp.zeros((o_ref.shape[0], o_ref.shape[1] - cols), o_ref.dtype)

    @pl.when(i >= n_in_blocks)
    def _():
        o_ref[...] = jnp.zeros(o_ref.shape, o_ref.dtype)


def _cast_pad(w_stack, layer, rows_out=None, cols_out=None, br=256):
    _, rows, cols = w_stack.shape
    rows_out = rows_out or rows
    cols_out = cols_out or cols
    nin = rows // br
    return pl.pallas_call(
        functools.partial(_cast_kernel, n_in_blocks=nin, cols=cols),
        grid=(rows_out // br,),
        in_specs=[pl.BlockSpec((None, br, cols), lambda i: (layer, jnp.minimum(i, nin - 1), 0))],
        out_specs=pl.BlockSpec((br, cols_out), lambda i: (i, 0)),
        out_shape=jax.ShapeDtypeStruct((rows_out, cols_out), BF16),
        compiler_params=_cparams(("parallel",)),
        name="cast_pad",
    )(w_stack)


def _mm_kernel(a_ref, w_ref, o_ref):
    o_ref[...] = _dot(a_ref[...], w_ref[...]).astype(o_ref.dtype)


def _mm(a, w, out_dtype, bm=1024, bn=1024, name="mm"):
    m, k = a.shape
    n = w.shape[1]
    return pl.pallas_call(
        _mm_kernel,
        grid=(m // bm, n // bn),
        in_specs=[pl.BlockSpec((bm, k), lambda i, j: (i, 0)),
                  pl.BlockSpec((k, bn), lambda i, j: (0, j))],
        out_specs=pl.BlockSpec((bm, bn), lambda i, j: (i, j)),
        out_shape=jax.ShapeDtypeStruct((m, n), out_dtype),
        compiler_params=_cparams(("parallel", "parallel")),
        name=name,
    )(a, w)


def _mm_res_kernel(a_ref, w_ref, r_ref, o_ref):
    o_ref[...] = r_ref[...] + _dot(a_ref[...], w_ref[...])


def _mm_res(a, w, res, bm=1024, bn=512, name="mm_res"):
    m, k = a.shape
    n = w.shape[1]
    return pl.pallas_call(
        _mm_res_kernel,
        grid=(m // bm, n // bn),
        in_specs=[pl.BlockSpec((bm, k), lambda i, j: (i, 0)),
                  pl.BlockSpec((k, bn), lambda i, j: (0, j)),
                  pl.BlockSpec((bm, bn), lambda i, j: (i, j))],
        out_specs=pl.BlockSpec((bm, bn), lambda i, j: (i, j)),
        out_shape=jax.ShapeDtypeStruct((m, n), F32),
        compiler_params=_cparams(("parallel", "parallel")),
        name=name,
    )(a, w, res)


def _mm_res2_kernel(a1_ref, a2_ref, w1_ref, w2_ref, r_ref, o_ref):
    o_ref[...] = r_ref[...] + _dot(a1_ref[...], w1_ref[...]) + _dot(a2_ref[...], w2_ref[...])


def _mm_res2(a1, a2, w, res, bm=1024, bn=512, name="mm_res2"):
    m, k1 = a1.shape
    k2 = a2.shape[1]
    n = w.shape[1]
    assert k1 % k2 == 0 and w.shape[0] == k1 + k2
    return pl.pallas_call(
        _mm_res2_kernel,
        grid=(m // bm, n // bn),
        in_specs=[pl.BlockSpec((bm, k1), lambda i, j: (i, 0)),
                  pl.BlockSpec((bm, k2), lambda i, j: (i, 0)),
                  pl.BlockSpec((k1, bn), lambda i, j: (0, j)),
                  pl.BlockSpec((k2, bn), lambda i, j: (k1 // k2, j)),
                  pl.BlockSpec((bm, bn), lambda i, j: (i, j))],
        out_specs=pl.BlockSpec((bm, bn), lambda i, j: (i, j)),
        out_shape=jax.ShapeDtypeStruct((m, n), F32),
        compiler_params=_cparams(("parallel", "parallel")),
        name=name,
    )(a1, a2, w, w, res)


def _mm_res_acc_kernel(a_ref, w_ref, r_ref, o_ref, acc_ref):
    k = pl.program_id(2)

    @pl.when(k == 0)
    def _():
        acc_ref[...] = jnp.zeros(acc_ref.shape, F32)

    acc_ref[...] += _dot(a_ref[...], w_ref[...])

    @pl.when(k == pl.num_programs(2) - 1)
    def _():
        o_ref[...] = r_ref[...] + acc_ref[...]


def _mm_res_acc(a, w, res, bk, bm=1024, bn=1024, name="mm_res_acc"):
    m, k = a.shape
    n = w.shape[1]
    return pl.pallas_call(
        _mm_res_acc_kernel,
        grid=(m // bm, n // bn, k // bk),
        in_specs=[pl.BlockSpec((bm, bk), lambda i, j, kk: (i, kk)),
                  pl.BlockSpec((bk, bn), lambda i, j, kk: (kk, j)),
                  pl.BlockSpec((bm, bn), lambda i, j, kk: (i, j))],
        out_specs=pl.BlockSpec((bm, bn), lambda i, j, kk: (i, j)),
        out_shape=jax.ShapeDtypeStruct((m, n), F32),
        scratch_shapes=[pltpu.VMEM((bm, bn), F32)],
        compiler_params=_cparams(("parallel", "parallel", "arbitrary")),
        name=name,
    )(a, w, res)


def _rope_tables(seq, rot, width):
    half = rot // 2
    inv_freq = 1.0 / (ROPE_THETA ** (jnp.arange(half, dtype=F32) * (2.0 / rot)))
    ang = jnp.arange(seq, dtype=jnp.int32).astype(F32)[:, None] * inv_freq[None, :]
    cos, sin = jnp.cos(ang), jnp.sin(ang)
    zh = jnp.zeros((seq, half), F32)
    rest = width - rot
    c = jnp.concatenate([cos, cos, jnp.ones((seq, rest), F32)], axis=1)
    sa = jnp.concatenate([zh, sin, jnp.zeros((seq, rest), F32)], axis=1)
    sb = jnp.concatenate([-sin, zh, jnp.zeros((seq, rest), F32)], axis=1)
    return c, sa, sb


def _apply_rope(y, c, sa, sb, half):
    width = y.shape[-1]
    return y * c + pltpu.roll(y, half, 1) * sa + pltpu.roll(y, width - half, 1) * sb


def _lat_kernel(a_ref, w_ref, gq_ref, gkv_ref, cq_ref, ckv_ref, kpe_ref):
    z = _dot(a_ref[...], w_ref[...])
    c1 = MLA_Q_RANK
    c2 = c1 + MLA_KV_RANK
    cq = z[:, :c1]
    ckv = z[:, c1:c2]
    cq = cq * lax.rsqrt(jnp.mean(cq * cq, axis=-1, keepdims=True) + NORM_EPS)
    ckv = ckv * lax.rsqrt(jnp.mean(ckv * ckv, axis=-1, keepdims=True) + NORM_EPS)
    cq_ref[...] = (cq * gq_ref[...]).astype(cq_ref.dtype)
    ckv_ref[...] = (ckv * gkv_ref[...]).astype(ckv_ref.dtype)
    kpe_ref[...] = z[:, c2:]


def _lat_proj(h, w_lat, gq, gkv, bm=512):
    m, k = h.shape
    return pl.pallas_call(
        _lat_kernel,
        grid=(m // bm,),
        in_specs=[pl.BlockSpec((bm, k), lambda i: (i, 0)),
                  pl.BlockSpec((k, LAT_PAD), lambda i: (0, 0)),
                  pl.BlockSpec((1, MLA_Q_RANK), lambda i: (0, 0)),
                  pl.BlockSpec((1, MLA_KV_RANK), lambda i: (0, 0))],
        out_specs=[pl.BlockSpec((bm, MLA_Q_RANK), lambda i: (i, 0)),
                   pl.BlockSpec((bm, MLA_KV_RANK), lambda i: (i, 0)),
                   pl.BlockSpec((bm, 128), lambda i: (i, 0))],
        out_shape=[jax.ShapeDtypeStruct((m, MLA_Q_RANK), BF16),
                   jax.ShapeDtypeStruct((m, MLA_KV_RANK), BF16),
                   jax.ShapeDtypeStruct((m, 128), F32)],
        compiler_params=_cparams(("parallel",)),
        name="lat_proj",
    )(h, w_lat, gq.reshape(1, -1), gkv.reshape(1, -1))


def _mla_q_kernel(a_ref, w_ref, g_ref, c_ref, sa_ref, sb_ref, o_ref, *, heads):
    z = _dot(a_ref[...], w_ref[...])
    g = g_ref[...]
    c, sa, sb = c_ref[...], sa_ref[...], sb_ref[...]
    for h in range(heads):
        zh = z[:, h * MLA_HEAD_PAD:(h + 1) * MLA_HEAD_PAD]
        ms = jnp.sum(zh * zh, axis=-1, keepdims=True) * (1.0 / MLA_QK)
        y = zh * lax.rsqrt(ms + NORM_EPS) * g
        o_ref[:, h * MLA_HEAD_PAD:h * MLA_HEAD_PAD + 128] = y[:, :128].astype(o_ref.dtype)
        yr = _apply_rope(y[:, 128:], c, sa, sb, MLA_ROPE // 2)
        o_ref[:, h * MLA_HEAD_PAD + 128:(h + 1) * MLA_HEAD_PAD] = yr.astype(o_ref.dtype)


def _mla_q_proj(cq, w_uq, g_pad, tabs, seq, bm=1024, heads=4):
    m, k = cq.shape
    n = w_uq.shape[1]
    bn = heads * MLA_HEAD_PAD
    nsb = seq // bm
    tab_spec = pl.BlockSpec((bm, 128), lambda i, j: (i % nsb, 0))
    return pl.pallas_call(
        functools.partial(_mla_q_kernel, heads=heads),
        grid=(m // bm, n // bn),
        in_specs=[pl.BlockSpec((bm, k), lambda i, j: (i, 0)),
                  pl.BlockSpec((k, bn), lambda i, j: (0, j)),
                  pl.BlockSpec((1, MLA_HEAD_PAD), lambda i, j: (0, 0)),
                  tab_spec, tab_spec, tab_spec],
        out_specs=pl.BlockSpec((bm, bn), lambda i, j: (i, j)),
        out_shape=jax.ShapeDtypeStruct((m, n), BF16),
        compiler_params=_cparams(("parallel", "parallel")),
        name="mla_q_proj",
    )(cq, w_uq, g_pad, *tabs)


def _mla_kv_kernel(a_ref, w_ref, kpe_ref, g_ref, c_ref, sa_ref, sb_ref, k_ref, v_ref, *, heads):
    z = _dot(a_ref[...], w_ref[...])
    kpe = kpe_ref[...]
    g = g_ref[...]
    g_nope, g_rope = g[:, :128], g[:, 128:]
    c, sa, sb = c_ref[...], sa_ref[...], sb_ref[...]
    ss_pe = jnp.sum(kpe * kpe, axis=-1, keepdims=True)
    one_col = (lax.broadcasted_iota(jnp.int32, (1, 128), 1) == 0).astype(v_ref.dtype)
    for h in range(heads):
        nope = z[:, h * 256:h * 256 + 128]
        ms = (jnp.sum(nope * nope, axis=-1, keepdims=True) + ss_pe) * (1.0 / MLA_QK)
        r = lax.rsqrt(ms + NORM_EPS)
        k_ref[:, h * MLA_HEAD_PAD:h * MLA_HEAD_PAD + 128] = (nope * r * g_nope).astype(k_ref.dtype)
        kr = _apply_rope(kpe * r * g_rope, c, sa, sb, MLA_ROPE // 2)
        k_ref[:, h * MLA_HEAD_PAD + 128:(h + 1) * MLA_HEAD_PAD] = kr.astype(k_ref.dtype)
        v_ref[:, h * MLA_V_PAD:h * MLA_V_PAD + MLA_V] = z[:, h * 256 + 128:(h + 1) * 256].astype(v_ref.dtype)
        v_ref[:, h * MLA_V_PAD + MLA_V:(h + 1) * MLA_V_PAD] = jnp.broadcast_to(one_col, (z.shape[0], 128))


def _mla_kv_proj(ckv, w_ukv, kpe, g_pad, tabs, seq, bm=1024, heads=4):
    m, k = ckv.shape
    nsb = seq // bm
    tab_spec = pl.BlockSpec((bm, 128), lambda i, j: (i % nsb, 0))
    return pl.pallas_call(
        functools.partial(_mla_kv_kernel, heads=heads),
        grid=(m // bm, MLA_HEADS // heads),
        in_specs=[pl.BlockSpec((bm, k), lambda i, j: (i, 0)),
                  pl.BlockSpec((k, heads * 256), lambda i, j: (0, j)),
                  pl.BlockSpec((bm, 128), lambda i, j: (i, 0)),
                  pl.BlockSpec((1, MLA_HEAD_PAD), lambda i, j: (0, 0)),
                  tab_spec, tab_spec, tab_spec],
        out_specs=[pl.BlockSpec((bm, heads * MLA_HEAD_PAD), lambda i, j: (i, j)),
                   pl.BlockSpec((bm, heads * MLA_V_PAD), lambda i, j: (i, j))],
        out_shape=[jax.ShapeDtypeStruct((m, MLA_HEADS * MLA_HEAD_PAD), BF16),
                   jax.ShapeDtypeStruct((m, MLA_HEADS * MLA_V_PAD), BF16)],
        compiler_params=_cparams(("parallel", "parallel")),
        name="mla_kv_proj",
    )(ckv, w_ukv, kpe, g_pad, *tabs)


def _heads_kernel(a_ref, w_ref, g_ref, o_ref, *, heads, n_norm):
    z = _dot(a_ref[...], w_ref[...])
    j = pl.program_id(1)

    @pl.when(j < n_norm)
    def _():
        g = g_ref[0]
        for h in range(heads):
            zh = z[:, h * 128:(h + 1) * 128]
            y = zh * lax.rsqrt(jnp.mean(zh * zh, axis=-1, keepdims=True) + NORM_EPS) * g
            o_ref[:, h * 128:(h + 1) * 128] = y.astype(o_ref.dtype)

    @pl.when(j >= n_norm)
    def _():
        o_ref[...] = z.astype(o_ref.dtype)


def _heads_proj(a, w, gains, n_q_blocks, bm=1024, bn=1024, name="heads_proj"):
    m, k = a.shape
    n = w.shape[1]
    g3 = jnp.concatenate([gains, jnp.ones((1, 128), F32)], axis=0).reshape(3, 1, 128)
    return pl.pallas_call(
        functools.partial(_heads_kernel, heads=bn // 128, n_norm=2 * n_q_blocks),
        grid=(m // bm, n // bn),
        in_specs=[pl.BlockSpec((bm, k), lambda i, j: (i, 0)),
                  pl.BlockSpec((k, bn), lambda i, j: (0, j)),
                  pl.BlockSpec((1, 1, 128), lambda i, j: (j // n_q_blocks, 0, 0))],
        out_specs=pl.BlockSpec((bm, bn), lambda i, j: (i, j)),
        out_shape=jax.ShapeDtypeStruct((m, n), BF16),
        compiler_params=_cparams(("parallel", "parallel")),
        name=name,
    )(a, w, g3)


def _flash_stages(q_ref, k_ref, v_ref, o_ref, s_w, s_r, p_w, p_r, al_w, al_r, mr_w, mr_r, m_ref,
                  acc_ref, *, total, nk):
    t = pl.program_id(0)
    s = _dot_nt(q_ref[...], k_ref[...])
    s_w[...] = s
    mr_w[...] = jnp.max(s, axis=-1, keepdims=True)

    valid_b = jnp.logical_and(t >= 1, t <= total)
    ki_b = jnp.clip(t - 1, 0, total - 1) % nk
    m_old = m_ref[...]
    m_prev = jnp.where(ki_b == 0, -jnp.inf, m_old)
    m_new = jnp.maximum(m_prev, mr_r[...])
    al_w[...] = jnp.exp2(m_prev - m_new)
    p_w[...] = jnp.exp2(s_r[...] - m_new).astype(p_w.dtype)
    m_ref[...] = jnp.where(valid_b, m_new, m_old)

    acc = al_r[...] * acc_ref[...] + _dot(p_r[...], v_ref[...])
    acc_ref[...] = acc
    ki_c = jnp.clip(t - 2, 0, total - 1) % nk

    @pl.when(jnp.logical_and(t >= 2, ki_c == nk - 1))
    def _():
        o_ref[...] = (acc[:, :MLA_V] / acc[:, MLA_V:MLA_V + 1]).astype(o_ref.dtype)


def _flash_kernel(q_ref, k_ref, v_ref, o_ref, s0, s1, p0, p1, al0, al1, mr0, mr1, m_ref, acc_ref, **kw):
    t = pl.program_id(0)
    s_scr, p_scr, al_scr, mr_scr = (s0, s1), (p0, p1), (al0, al1), (mr0, mr1)

    @pl.when(t == 0)
    def _():
        for ref in (s0, s1, p0, p1, al0, al1, mr0, mr1, m_ref, acc_ref):
            ref[...] = jnp.zeros(ref.shape, ref.dtype)

    for par in (0, 1):
        @pl.when(t % 2 == par)
        def _(par=par):
            _flash_stages(q_ref, k_ref, v_ref, o_ref, s_scr[par], s_scr[1 - par],
                          p_scr[1 - par], p_scr[par], al_scr[1 - par], al_scr[par],
                          mr_scr[par], mr_scr[1 - par], m_ref, acc_ref, **kw)


def _mla_attention(q, k, v, batch, seq, tq=1024, tk=2048):
    m = q.shape[0]
    tk = min(tk, seq // 2)
    nq, nk = seq // tq, seq // tk
    total = batch * MLA_HEADS * nq * nk

    def unravel(tt):
        ki = tt % nk
        r = tt // nk
        qi = r % nq
        r = r // nq
        return r // MLA_HEADS, r % MLA_HEADS, qi, ki

    def q_map(t):
        b, h, qi, _ = unravel(jnp.minimum(t, total - 1))
        return (b * nq + qi, h)

    def k_map(t):
        b, h, _, ki = unravel(jnp.minimum(t, total - 1))
        return (b * nk + ki, h)

    def v_map(t):
        b, h, _, ki = unravel(jnp.clip(t - 2, 0, total - 1))
        return (b * nk + ki, h)

    def o_map(t):
        b, h, qi, _ = unravel(jnp.clip(t - 2, 0, total - 1))
        return (b * nq + qi, h)

    def col():
        return pltpu.VMEM((tq, 1), F32)

    return pl.pallas_call(
        functools.partial(_flash_kernel, total=total, nk=nk),
        grid=(total + 2,),
        in_specs=[pl.BlockSpec((tq, MLA_HEAD_PAD), q_map),
                  pl.BlockSpec((tk, MLA_HEAD_PAD), k_map),
                  pl.BlockSpec((tk, MLA_V_PAD), v_map)],
        out_specs=pl.BlockSpec((tq, MLA_V), o_map),
        out_shape=jax.ShapeDtypeStruct((m, MLA_HEADS * MLA_V), BF16),
        scratch_shapes=[pltpu.VMEM((tq, tk), F32), pltpu.VMEM((tq, tk), F32),
                        pltpu.VMEM((tq, tk), BF16), pltpu.VMEM((tq, tk), BF16),
                        col(), col(), col(), col(), col(), pltpu.VMEM((tq, MLA_V_PAD), F32)],
        compiler_params=_cparams(("arbitrary",)),
        name="mla_attention",
    )(q, k, v)


DIL_SQ = 256
DIL_SK = DIL_SQ + 2 * DIL_RADIUS


def _dil_kernel(q_ref, kp_ref, k_ref, kn_ref, vp_ref, v_ref, vn_ref, o_ref, st_ref,
                kwin_ref, vwin_ref, *, tq, length):
    r = DIL_RADIUS
    q0 = pl.program_id(2) * tq
    kwin_ref[0:r] = kp_ref[...]
    kwin_ref[r:r + tq] = k_ref[...]
    kwin_ref[r + tq:] = kn_ref[...]
    vwin_ref[0:r] = vp_ref[...]
    vwin_ref[r:r + tq] = v_ref[...]
    vwin_ref[r + tq:] = vn_ref[...]
    qi = lax.broadcasted_iota(jnp.int32, (DIL_SQ, DIL_SK), 0)
    kj = lax.broadcasted_iota(jnp.int32, (DIL_SQ, DIL_SK), 1)
    band = (kj >= qi) & (kj <= qi + 2 * r)
    lane = lax.broadcasted_iota(jnp.int32, (1, 128), 1)
    for c in range(tq // DIL_SQ):
        kpos = q0 + c * DIL_SQ - r + kj
        valid = band & (kpos >= 0) & (kpos < length)
        st = jnp.zeros((DIL_SQ, 128), F32)
        for h in range(DIL_HPG):
            hs = slice(h * 128, (h + 1) * 128)
            qh = q_ref[c * DIL_SQ:(c + 1) * DIL_SQ, hs]
            kh = kwin_ref[c * DIL_SQ:c * DIL_SQ + DIL_SK, hs]
            vh = vwin_ref[c * DIL_SQ:c * DIL_SQ + DIL_SK, hs]
            s = jnp.where(valid, _dot_nt(qh, kh), NEG_INF)
            mx = jnp.max(s, axis=-1, keepdims=True)
            p = jnp.exp2(s - mx)
            den = jnp.sum(p, axis=-1, keepdims=True)
            o = _dot(p.astype(vh.dtype), vh) / den
            o_ref[c * DIL_SQ:(c + 1) * DIL_SQ, hs] = o
            st = jnp.where(lane == h, mx, st)
            st = jnp.where(lane == DIL_HPG + h, den, st)
        st_ref[c * DIL_SQ:(c + 1) * DIL_SQ, :] = st


def _dil_proj_kernel(a_ref, w_ref, g_ref, c_ref, sa_ref, sb_ref, o_ref, y_scr, *, dil, heads):
    z = _dot(a_ref[...], w_ref[...])
    part = pl.program_id(1)
    rows = z.shape[0] // dil

    @pl.when(part < 2)
    def _():
        g = g_ref[0]
        for h in range(heads):
            zh = z[:, h * 128:(h + 1) * 128]
            y = zh * lax.rsqrt(jnp.mean(zh * zh, axis=-1, keepdims=True) + NORM_EPS) * g
            y_scr[h] = _apply_rope(y, c_ref[...], sa_ref[...], sb_ref[...], DIL_ROT // 2)

    @pl.when(part == 2)
    def _():
        for h in range(heads):
            y_scr[h] = z[:, h * 128:(h + 1) * 128]

    for h in range(heads):
        hs = slice(h * 128, (h + 1) * 128)
        if dil == 1:
            o_ref[0, :, hs] = y_scr[h].astype(o_ref.dtype)
        else:
            for r in range(dil):
                o_ref[r, :, hs] = y_scr[h, pl.ds(r, rows, stride=dil), :].astype(o_ref.dtype)


def _dil_proj(h, w_b, gains, tabs, batch, seq, group, dil, bm=1024):
    m, k = h.shape
    w = DIL_HPG * 128
    nsb = seq // bm
    g3 = jnp.concatenate([gains, jnp.ones((1, 128), F32)], axis=0).reshape(3, 1, 128)
    tab_spec = pl.BlockSpec((bm, 128), lambda i, p: (i % nsb, 0))
    return pl.pallas_call(
        functools.partial(_dil_proj_kernel, dil=dil, heads=DIL_HPG),
        grid=(m // bm, 3),
        in_specs=[pl.BlockSpec((bm, k), lambda i, p: (i, 0)),
                  pl.BlockSpec((k, w), lambda i, p: (0, p * 3 + group)),
                  pl.BlockSpec((1, 1, 128), lambda i, p: (p, 0, 0)),
                  tab_spec, tab_spec, tab_spec],
        out_specs=pl.BlockSpec((None, dil, bm // dil, w), lambda i, p: (i // nsb, 0, i % nsb, p)),
        out_shape=jax.ShapeDtypeStruct((batch, dil, seq // dil, 3 * w), BF16),
        scratch_shapes=[pltpu.VMEM((DIL_HPG, bm, 128), F32)],
        compiler_params=_cparams(("parallel", "arbitrary")),
        name=f"dil_proj_g{group}",
    )(h, w_b, g3, *tabs)


def _dilated_group(qkv, group):
    batch, dil, length, _ = qkv.shape
    tq = min(512, length)
    r = DIL_RADIUS
    nq = length // tq
    hb = tq // r
    nhb = length // r
    w = DIL_HPG * 128

    def main(part):
        return pl.BlockSpec((None, None, tq, w), lambda b, rr, qi: (b, rr, qi, part))

    def prev(part):
        return pl.BlockSpec((None, None, r, w),
                            lambda b, rr, qi: (b, rr, jnp.maximum(qi * hb - 1, 0), part))

    def nxt(part):
        return pl.BlockSpec((None, None, r, w),
                            lambda b, rr, qi: (b, rr, jnp.minimum((qi + 1) * hb, nhb - 1), part))

    return pl.pallas_call(
        functools.partial(_dil_kernel, tq=tq, length=length),
        grid=(batch, dil, nq),
        in_specs=[main(0), prev(1), main(1), nxt(1), prev(2), main(2), nxt(2)],
        out_specs=[pl.BlockSpec((None, None, tq, w), lambda b, rr, qi: (b, rr, qi, 0)),
                   pl.BlockSpec((None, None, tq, 128), lambda b, rr, qi: (b, rr, qi, 0))],
        out_shape=[jax.ShapeDtypeStruct((batch, dil, length, w), F32),
                   jax.ShapeDtypeStruct((batch, dil, length, 128), F32)],
        scratch_shapes=[pltpu.VMEM((tq + 2 * r, w), BF16), pltpu.VMEM((tq + 2 * r, w), BF16)],
        compiler_params=_cparams(("parallel", "parallel", "parallel")),
        name=f"dilated_g{group}",
    )(qkv, qkv, qkv, qkv, qkv, qkv, qkv)


def _dil_merge_kernel(o0_ref, o1_ref, o2_ref, s0_ref, s1_ref, s2_ref, out_ref, on_scr, sn_scr,
                      *, dils):
    bm = out_ref.shape[0]
    for gi, (o_ref, s_ref) in enumerate(((o0_ref, s0_ref), (o1_ref, s1_ref), (o2_ref, s2_ref))):
        dil = dils[gi]
        for r in range(dil):
            rows = pl.ds(r, bm // dil, stride=dil) if dil > 1 else slice(None)
            sn_scr[gi, rows, :] = s_ref[r]
            for h in range(DIL_HPG):
                on_scr[gi, h, rows, :] = o_ref[r, :, h * 128:(h + 1) * 128]
    s0, s1, s2 = sn_scr[0], sn_scr[1], sn_scr[2]
    for h in range(DIL_HPG):
        hs = slice(h * 128, (h + 1) * 128)
        m0, m1, m2 = s0[:, h:h + 1], s1[:, h:h + 1], s2[:, h:h + 1]
        l0, l1, l2 = (s0[:, DIL_HPG + h:DIL_HPG + h + 1], s1[:, DIL_HPG + h:DIL_HPG + h + 1],
                      s2[:, DIL_HPG + h:DIL_HPG + h + 1])
        mx = jnp.maximum(jnp.maximum(m0, m1), m2)
        w0 = jnp.exp2(m0 - mx) * l0
        w1 = jnp.exp2(m1 - mx) * l1
        w2 = jnp.exp2(m2 - mx) * l2
        num = w0 * on_scr[0, h] + w1 * on_scr[1, h] + w2 * on_scr[2, h]
        out_ref[:, hs] = (num / (w0 + w1 + w2)).astype(out_ref.dtype)


def _dil_merge(outs, stats, batch, seq, bm=512):
    w = DIL_HPG * 128
    nsb = seq // bm
    dils = tuple(o.shape[1] for o in outs)

    def spec(dil, width):
        return pl.BlockSpec((None, dil, bm // dil, width), lambda b, i: (b, 0, i, 0))

    return pl.pallas_call(
        functools.partial(_dil_merge_kernel, dils=dils),
        grid=(batch, nsb),
        in_specs=[spec(d, w) for d in dils] + [spec(d, 128) for d in dils],
        out_specs=pl.BlockSpec((bm, w), lambda b, i: (b * nsb + i, 0)),
        out_shape=jax.ShapeDtypeStruct((batch * seq, w), BF16),
        scratch_shapes=[pltpu.VMEM((3, DIL_HPG, bm, 128), F32), pltpu.VMEM((3, bm, 128), F32)],
        compiler_params=_cparams(("parallel", "parallel")),
        name="dilated_merge",
    )(*outs, *stats)


NA_TQ = NA_QROWS * GRID_W
NA_TK = NA_KROWS * GRID_W
NA_HPS = 8


def _na_bias_tables(rpb, rows):
    heads = rpb.shape[0]
    cols = np.arange(GRID_W)
    cs = np.clip(cols - NA_COLS // 2, 0, GRID_W - NA_COLS)
    kc = np.arange(GRID_W)
    col_ok = (kc[None, :] >= cs[:, None]) & (kc[None, :] < cs[:, None] + NA_COLS)
    col_idx = np.clip(kc[None, :] - cols[:, None] + (NA_COLS - 1), 0, 2 * NA_COLS - 2)
    toe = jnp.where(col_ok[None, None], (rpb * LOG2E)[:, :, col_idx], NEG_INF)
    offs, oks = [], []
    for i0, ks in ((0, 0), (NA_QROWS * 2, NA_QROWS), (rows - NA_QROWS, rows - NA_KROWS)):
        qi = i0 + np.arange(NA_QROWS)
        rs = np.clip(qi - NA_ROWS // 2, 0, rows - NA_ROWS)
        kr = ks + np.arange(NA_KROWS)
        oks.append((kr[None, :] >= rs[:, None]) & (kr[None, :] < rs[:, None] + NA_ROWS))
        offs.append(ks - qi + (NA_ROWS - 1))
    offs, oks = np.stack(offs), np.stack(oks)
    idx = offs[:, :, None] + np.arange(NA_KROWS)[None, None, :]
    assert idx[oks].min() >= 0 and idx[oks].max() <= 2 * NA_ROWS - 2
    wide = jnp.concatenate([toe, toe], axis=-1)

    def table_kernel(t_ref, o_ref):
        left = lax.broadcasted_iota(jnp.int32, (GRID_W, 2 * GRID_W), 1) < GRID_W
        masked = jnp.full((GRID_W, 2 * GRID_W), NEG_INF, F32)
        for kind in range(3):
            @pl.when(pl.program_id(1) == kind)
            def _(kind=kind):
                for q in range(NA_QROWS):
                    for pr in range(NA_KROWS // 2):
                        a, b = 2 * pr, 2 * pr + 1
                        ta = t_ref[int(idx[kind, q, a])] if oks[kind, q, a] else masked
                        tb = t_ref[int(idx[kind, q, b])] if oks[kind, q, b] else masked
                        o_ref[q * GRID_W:(q + 1) * GRID_W, pr * 2 * GRID_W:(pr + 1) * 2 * GRID_W] = (
                            jnp.where(left, ta, tb))

    return pl.pallas_call(
        table_kernel,
        grid=(heads, 3),
        in_specs=[pl.BlockSpec((None, 2 * NA_ROWS - 1, GRID_W, 2 * GRID_W), lambda h, kind: (h, 0, 0, 0))],
        out_specs=pl.BlockSpec((None, None, NA_TQ, NA_TK), lambda h, kind: (kind, h, 0, 0)),
        out_shape=jax.ShapeDtypeStruct((3, heads, NA_TQ, NA_TK), F32),
        compiler_params=_cparams(("parallel", "arbitrary")),
        name="na_bias_table",
    )(wide)


def _na_kernel(q_ref, k0_ref, k1_ref, k2_ref, v0_ref, v1_ref, v2_ref, b_ref, o_ref):
    for h in range(NA_HPS):
        hs = slice(h * 128, (h + 1) * 128)
        q = q_ref[0, :, hs]
        k = jnp.concatenate([k0_ref[0, :, hs], k1_ref[0, :, hs], k2_ref[0, :, hs]], axis=0)
        v = jnp.concatenate([v0_ref[0, :, hs], v1_ref[0, :, hs], v2_ref[0, :, hs]], axis=0)
        s = _dot_nt(q, k) + b_ref[0, h]
        mx = jnp.max(s, axis=-1, keepdims=True)
        p = jnp.exp2(s - mx)
        den = jnp.sum(p, axis=-1, keepdims=True)
        o = _dot(p.astype(v.dtype), v) / den
        o_ref[0, :, hs] = o.astype(o_ref.dtype)


def _na_attention(qkv, bias, batch, seq):
    rows = seq // GRID_W
    nblk = rows // NA_QROWS
    nkb = NA_KROWS // NA_QROWS
    ng = NA_HEADS // NA_HPS
    w = NA_HPS * 128
    qv = qkv.reshape(batch, seq, qkv.shape[1])

    def kv_spec(part, off):
        return pl.BlockSpec(
            (1, NA_TQ, w),
            lambda b, g, rb: (b, jnp.clip(rb - 1, 0, nblk - nkb) + off, part * ng + g))

    def bias_map(b, g, rb):
        cfg = jnp.where(rb == 0, 0, jnp.where(rb == nblk - 1, 2, 1))
        return (cfg, g, 0, 0)

    o = pl.pallas_call(
        _na_kernel,
        grid=(batch, ng, nblk),
        in_specs=[pl.BlockSpec((1, NA_TQ, w), lambda b, g, rb: (b, rb, g)),
                  kv_spec(1, 0), kv_spec(1, 1), kv_spec(1, 2),
                  kv_spec(2, 0), kv_spec(2, 1), kv_spec(2, 2),
                  pl.BlockSpec((1, NA_HPS, NA_TQ, NA_TK), bias_map)],
        out_specs=pl.BlockSpec((1, NA_TQ, w), lambda b, g, rb: (b, rb, g)),
        out_shape=jax.ShapeDtypeStruct((batch, seq, NA_HEADS * 128), BF16),
        compiler_params=_cparams(("parallel", "parallel", "arbitrary")),
        name="na_attention",
    )(qv, qv, qv, qv, qv, qv, qv, bias)
    return o.reshape(batch * seq, NA_HEADS * 128)


def _ffn_up_kernel(h_ref, w_ref, a_ref, ap_ref, an_ref, cw_ref, cb_ref, o_ref, *, bm, seq):
    i = pl.program_id(0)
    u = _dot(h_ref[...], w_ref[...])
    a = a_ref[...]
    first = (i * bm) % seq == 0
    last = ((i + 1) * bm) % seq == 0
    prev_row = jnp.where(first, 0.0, ap_ref[7:8, :])
    next_row = jnp.where(last, 0.0, an_ref[0:1, :])
    ridx = lax.broadcasted_iota(jnp.int32, (bm, 1), 0)
    a_prev = jnp.where(ridx == 0, prev_row, pltpu.roll(a, 1, 0))
    a_next = jnp.where(ridx == bm - 1, next_row, pltpu.roll(a, bm - 1, 0))
    cw = cw_ref[...]
    ac = a_prev * cw[0:1] + a * cw[1:2] + a_next * cw[2:3] + cb_ref[...]
    o_ref[...] = (ac * jax.nn.sigmoid(ac) * u).astype(o_ref.dtype)


def _ffn_up(h, w_up, a, conv_w, conv_b, seq, bm=1024, bn=512):
    m, k = h.shape
    n = w_up.shape[1]
    rb = bm // 8
    nrb = m // 8
    return pl.pallas_call(
        functools.partial(_ffn_up_kernel, bm=bm, seq=seq),
        grid=(m // bm, n // bn),
        in_specs=[pl.BlockSpec((bm, k), lambda i, j: (i, 0)),
                  pl.BlockSpec((k, bn), lambda i, j: (0, j)),
                  pl.BlockSpec((bm, bn), lambda i, j: (i, j)),
                  pl.BlockSpec((8, bn), lambda i, j: (jnp.maximum(i * rb - 1, 0), j)),
                  pl.BlockSpec((8, bn), lambda i, j: (jnp.minimum((i + 1) * rb, nrb - 1), j)),
                  pl.BlockSpec((3, bn), lambda i, j: (0, j)),
                  pl.BlockSpec((1, bn), lambda i, j: (0, j))],
        out_specs=pl.BlockSpec((bm, bn), lambda i, j: (i, j)),
        out_shape=jax.ShapeDtypeStruct((m, n), BF16),
        compiler_params=_cparams(("parallel", "parallel")),
        name="ffn_up",
    )(h, w_up, a, a, a, conv_w, conv_b)


def _conv_ffn(x2d, seq, norm_g, w_gate, w_up, conv_w, conv_b, w_down, layer):
    pad = D_FF_PAD - D_FF
    wg = _cast_pad(w_gate, layer, cols_out=D_FF_PAD)
    wu = _cast_pad(w_up, layer, cols_out=D_FF_PAD)
    wd = _cast_pad(w_down, layer, rows_out=D_FF_PAD)
    cw = jnp.pad(conv_w, ((0, 0), (0, pad)))
    cb = jnp.pad(conv_b, ((0, pad),)).reshape(1, D_FF_PAD)
    h = _rmsnorm(x2d, norm_g)
    a = _mm(h, wg, F32, name="ffn_gate")
    g = _ffn_up(h, wu, a, cw, cb, seq)
    return _mm_res_acc(g, wd, x2d, bk=D_FF_PAD // 4, name="ffn_down")


def _even_layer(x2d, batch, seq, e_norm, w_in, q_lora_g, kv_lora_g, w_uq, w_ukv, mla_qn, mla_kn,
                dil_qn, dil_kn, w_out, layer):
    c3 = MLA_Q_RANK + MLA_KV_RANK + MLA_ROPE
    w_lat = jnp.pad(w_in[:, :c3], ((0, 0), (0, LAT_PAD - c3))).astype(BF16)
    w_b = w_in[:, c3:].astype(BF16)
    hpad = MLA_HEAD_PAD - MLA_QK
    w_uq_p = jnp.pad(w_uq.reshape(MLA_Q_RANK, MLA_HEADS, MLA_QK), ((0, 0), (0, 0), (0, hpad)))
    w_uq_p = w_uq_p.reshape(MLA_Q_RANK, MLA_HEADS * MLA_HEAD_PAD).astype(BF16)
    gq_pad = jnp.pad(mla_qn * (MLA_QK ** -0.5 * LOG2E), (0, hpad)).reshape(1, MLA_HEAD_PAD)
    gk_pad = jnp.pad(mla_kn, (0, hpad)).reshape(1, MLA_HEAD_PAD)
    mla_tabs = _rope_tables(seq, MLA_ROPE, 128)
    dil_tabs = _rope_tables(seq, DIL_ROT, 128)

    h = _rmsnorm(x2d, e_norm)
    cq, ckv, kpe = _lat_proj(h, w_lat, q_lora_g, kv_lora_g)
    q = _mla_q_proj(cq, w_uq_p, gq_pad, mla_tabs, seq)
    k, v = _mla_kv_proj(ckv, w_ukv.astype(BF16), kpe, gk_pad, mla_tabs, seq)
    o_a = _mla_attention(q, k, v, batch, seq)

    outs, stats = [], []
    for g, (window, dil) in enumerate(DIL_CONFIGS):
        assert window // (2 * dil) == DIL_RADIUS
        qkv_g = _dil_proj(h, w_b, jnp.stack([dil_qn * (HEAD_DIM ** -0.5 * LOG2E), dil_kn]), dil_tabs,
                          batch, seq, g, dil)
        o, st = _dilated_group(qkv_g, g)
        outs.append(o)
        stats.append(st)
    o_b = _dil_merge(outs, stats, batch, seq)
    return _mm_res2(o_a, o_b, _cast_pad(w_out, layer), x2d, name="even_out_proj")


def _odd_layer(x2d, batch, seq, o_norm, w_qkv, qn, kn, rpb, w_out, layer):
    h = _rmsnorm(x2d, o_norm)
    qkv = _heads_proj(h, _cast_pad(w_qkv, layer),
                      jnp.stack([qn * (HEAD_DIM ** -0.5 * LOG2E), kn]), NA_HEADS * 128 // 1024,
                      name="na_qkv_proj")
    bias = _na_bias_tables(rpb, seq // GRID_W)
    o = _na_attention(qkv, bias, batch, seq)
    return _mm_res(o, _cast_pad(w_out, layer), x2d, name="odd_out_proj")


def kernel(x, e_norm, e_w_in, e_q_lora_norm, e_kv_lora_norm, e_w_uq, e_w_ukv, e_mla_q_norm,
           e_mla_k_norm, e_dil_q_norm, e_dil_k_norm, e_w_out, o_norm, o_w_qkv, o_q_norm, o_k_norm,
           o_rpb, o_w_out, f_norm, f_w_gate, f_w_up, f_conv_w, f_conv_b, f_w_down):
    batch, seq, d = x.shape
    x2d = x.reshape(batch * seq, d)
    depth = f_norm.shape[0]
    for layer in range(depth):
        i = layer // 2
        if layer % 2 == 0:
            x2d = _even_layer(x2d, batch, seq, e_norm[i], e_w_in[i], e_q_lora_norm[i],
                              e_kv_lora_norm[i], e_w_uq[i], e_w_ukv[i], e_mla_q_norm[i],
                              e_mla_k_norm[i], e_dil_q_norm[i], e_dil_k_norm[i], e_w_out, i)
        else:
            x2d = _odd_layer(x2d, batch, seq, o_norm[i], o_w_qkv, o_q_norm[i], o_k_norm[i],
                             o_rpb[i], o_w_out, i)
        x2d = _conv_ffn(x2d, seq, f_norm[layer], f_w_gate, f_w_up, f_conv_w[layer],
                        f_conv_b[layer], f_w_down, layer)
    return x2d.reshape(batch, seq, d)
```

```python
import functools

import jax
import jax.numpy as jnp
import numpy as np
from jax import lax
from jax.experimental import pallas as pl
from jax.experimental.pallas import tpu as pltpu

F32 = jnp.float32
BF16 = jnp.bfloat16

HEAD_DIM = 128
ROPE_THETA = 500000.0
NORM_EPS = 1e-6
NEG_INF = -1e30

MLA_HEADS = 16
MLA_Q_RANK = 896
MLA_KV_RANK = 512
MLA_NOPE = 128
MLA_ROPE = 64
MLA_QK = MLA_NOPE + MLA_ROPE
MLA_V = 128
MLA_HEAD_PAD = 256
MLA_V_PAD = 256
LOG2E = float(np.log2(np.e))
LAT_PAD = 1536

DIL_CONFIGS = ((128, 1), (512, 4), (2048, 16))
DIL_HPG = 8
DIL_HEADS = 24
DIL_ROT = HEAD_DIM // 4
DIL_RADIUS = 64

GRID_W = 64
NA_HEADS = 32
NA_ROWS = 8
NA_COLS = 16
NA_QROWS = 4
NA_KROWS = 12

D_FF = 11008
D_FF_PAD = 11264

VMEM_LIMIT_BYTES = 56 * 1024 * 1024


def _cparams(sem):
    return pltpu.CompilerParams(dimension_semantics=sem, vmem_limit_bytes=VMEM_LIMIT_BYTES)


def _dot(a, b):
    return jnp.dot(a, b, preferred_element_type=F32)


def _dot_nt(a, b):
    return lax.dot_general(a, b, (((1,), (1,)), ((), ())), preferred_element_type=F32)


def _rmsnorm_kernel(x_ref, g_ref, o_ref):
    x = x_ref[...]
    y = x * lax.rsqrt(jnp.mean(x * x, axis=-1, keepdims=True) + NORM_EPS)
    o_ref[...] = (y * g_ref[...]).astype(o_ref.dtype)


def _rmsnorm(x2d, g, bm=512):
    m, d = x2d.shape
    return pl.pallas_call(
        _rmsnorm_kernel,
        grid=(m // bm,),
        in_specs=[pl.BlockSpec((bm, d), lambda i: (i, 0)),
                  pl.BlockSpec((1, d), lambda i: (0, 0))],
        out_specs=pl.BlockSpec((bm, d), lambda i: (i, 0)),
        out_shape=jax.ShapeDtypeStruct((m, d), BF16),
        compiler_params=_cparams(("parallel",)),
        name="rmsnorm",
    )(x2d, g.reshape(1, d))


def _cast_kernel(w_ref, o_ref, *, n_in_blocks, cols):
    i = pl.program_id(0)

    @pl.when(i < n_in_blocks)
    def _():
        o_ref[:, :cols] = w_ref[...].astype(o_ref.dtype)
        if o_ref.shape[1] > cols:
            o_ref[:, cols:] = jnp.zeros((o_ref.shape[0], o_ref.shape[1] - cols), o_ref.dtype)

    @pl.when(i >= n_in_blocks)
    def _():
        o_ref[...] = jnp.zeros(o_ref.shape, o_ref.dtype)


def _cast_pad(w_stack, layer, rows_out=None, cols_out=None, br=256):
    _, rows, cols = w_stack.shape
    rows_out = rows_out or rows
    cols_out = cols_out or cols
    nin = rows // br
    return pl.pallas_call(
        functools.partial(_cast_kernel, n_in_blocks=nin, cols=cols),
        grid=(rows_out // br,),
        in_specs=[pl.BlockSpec((None, br, cols), lambda i: (layer, jnp.minimum(i, nin - 1), 0))],
        out_specs=pl.BlockSpec((br, cols_out), lambda i: (i, 0)),
        out_shape=jax.ShapeDtypeStruct((rows_out, cols_out), BF16),
        compiler_params=_cparams(("parallel",)),
        name="cast_pad",
    )(w_stack)


def _mm_kernel(a_ref, w_ref, o_ref):
    o_ref[...] = _dot(a_ref[...], w_ref[...]).astype(o_ref.dtype)


def _mm(a, w, out_dtype, bm=1024, bn=1024, name="mm"):
    m, k = a.shape
    n = w.shape[1]
    return pl.pallas_call(
        _mm_kernel,
        grid=(m // bm, n // bn),
        in_specs=[pl.BlockSpec((bm, k), lambda i, j: (i, 0)),
                  pl.BlockSpec((k, bn), lambda i, j: (0, j))],
        out_specs=pl.BlockSpec((bm, bn), lambda i, j: (i, j)),
        out_shape=jax.ShapeDtypeStruct((m, n), out_dtype),
        compiler_params=_cparams(("parallel", "parallel")),
        name=name,
    )(a, w)


def _mm_res_kernel(a_ref, w_ref, r_ref, o_ref):
    o_ref[...] = r_ref[...] + _dot(a_ref[...], w_ref[...])


def _mm_res(a, w, res, bm=1024, bn=512, name="mm_res"):
    m, k = a.shape
    n = w.shape[1]
    return pl.pallas_call(
        _mm_res_kernel,
        grid=(m // bm, n // bn),
        in_specs=[pl.BlockSpec((bm, k), lambda i, j: (i, 0)),
                  pl.BlockSpec((k, bn), lambda i, j: (0, j)),
                  pl.BlockSpec((bm, bn), lambda i, j: (i, j))],
        out_specs=pl.BlockSpec((bm, bn), lambda i, j: (i, j)),
        out_shape=jax.ShapeDtypeStruct((m, n), F32),
        compiler_params=_cparams(("parallel", "parallel")),
        name=name,
    )(a, w, res)


def _mm_res2_kernel(a1_ref, a2_ref, w1_ref, w2_ref, r_ref, o_ref):
    o_ref[...] = r_ref[...] + _dot(a1_ref[...], w1_ref[...]) + _dot(a2_ref[...], w2_ref[...])


def _mm_res2(a1, a2, w, res, bm=1024, bn=512, name="mm_res2"):
    m, k1 = a1.shape
    k2 = a2.shape[1]
    n = w.shape[1]
    assert k1 % k2 == 0 and w.shape[0] == k1 + k2
    return pl.pallas_call(
        _mm_res2_kernel,
        grid=(m // bm, n // bn),
        in_specs=[pl.BlockSpec((bm, k1), lambda i, j: (i, 0)),
                  pl.BlockSpec((bm, k2), lambda i, j: (i, 0)),
                  pl.BlockSpec((k1, bn), lambda i, j: (0, j)),
                  pl.BlockSpec((k2, bn), lambda i, j: (k1 // k2, j)),
                  pl.BlockSpec((bm, bn), lambda i, j: (i, j))],
        out_specs=pl.BlockSpec((bm, bn), lambda i, j: (i, j)),
        out_shape=jax.ShapeDtypeStruct((m, n), F32),
        compiler_params=_cparams(("parallel", "parallel")),
        name=name,
    )(a1, a2, w, w, res)


def _mm_res_acc_kernel(a_ref, w_ref, r_ref, o_ref, acc_ref):
    k = pl.program_id(2)

    @pl.when(k == 0)
    def _():
        acc_ref[...] = jnp.zeros(acc_ref.shape, F32)

    acc_ref[...] += _dot(a_ref[...], w_ref[...])

    @pl.when(k == pl.num_programs(2) - 1)
    def _():
        o_ref[...] = r_ref[...] + acc_ref[...]


def _mm_res_acc(a, w, res, bk, bm=1024, bn=1024, name="mm_res_acc"):
    m, k = a.shape
    n = w.shape[1]
    return pl.pallas_call(
        _mm_res_acc_kernel,
        grid=(m // bm, n // bn, k // bk),
        in_specs=[pl.BlockSpec((bm, bk), lambda i, j, kk: (i, kk)),
                  pl.BlockSpec((bk, bn), lambda i, j, kk: (kk, j)),
                  pl.BlockSpec((bm, bn), lambda i, j, kk: (i, j))],
        out_specs=pl.BlockSpec((bm, bn), lambda i, j, kk: (i, j)),
        out_shape=jax.ShapeDtypeStruct((m, n), F32),
        scratch_shapes=[pltpu.VMEM((bm, bn), F32)],
        compiler_params=_cparams(("parallel", "parallel", "arbitrary")),
        name=name,
    )(a, w, res)


def _rope_tables(seq, rot, width):
    half = rot // 2
    inv_freq = 1.0 / (ROPE_THETA ** (jnp.arange(half, dtype=F32) * (2.0 / rot)))
    ang = jnp.arange(seq, dtype=jnp.int32).astype(F32)[:, None] * inv_freq[None, :]
    cos, sin = jnp.cos(ang), jnp.sin(ang)
    zh = jnp.zeros((seq, half), F32)
    rest = width - rot
    c = jnp.concatenate([cos, cos, jnp.ones((seq, rest), F32)], axis=1)
    sa = jnp.concatenate([zh, sin, jnp.zeros((seq, rest), F32)], axis=1)
    sb = jnp.concatenate([-sin, zh, jnp.zeros((seq, rest), F32)], axis=1)
    return c, sa, sb


def _apply_rope(y, c, sa, sb, half):
    width = y.shape[-1]
    return y * c + pltpu.roll(y, half, 1) * sa + pltpu.roll(y, width - half, 1) * sb


def _lat_kernel(a_ref, w_ref, gq_ref, gkv_ref, cq_ref, ckv_ref, kpe_ref):
    z = _dot(a_ref[...], w_ref[...])
    c1 = MLA_Q_RANK
    c2 = c1 + MLA_KV_RANK
    cq = z[:, :c1]
    ckv = z[:, c1:c2]
    cq = cq * lax.rsqrt(jnp.mean(cq * cq, axis=-1, keepdims=True) + NORM_EPS)
    ckv = ckv * lax.rsqrt(jnp.mean(ckv * ckv, axis=-1, keepdims=True) + NORM_EPS)
    cq_ref[...] = (cq * gq_ref[...]).astype(cq_ref.dtype)
    ckv_ref[...] = (ckv * gkv_ref[...]).astype(ckv_ref.dtype)
    kpe_ref[...] = z[:, c2:]


def _lat_proj(h, w_lat, gq, gkv, bm=512):
    m, k = h.shape
    return pl.pallas_call(
        _lat_kernel,
        grid=(m // bm,),
        in_specs=[pl.BlockSpec((bm, k), lambda i: (i, 0)),
                  pl.BlockSpec((k, LAT_PAD), lambda i: (0, 0)),
                  pl.BlockSpec((1, MLA_Q_RANK), lambda i: (0, 0)),
                  pl.BlockSpec((1, MLA_KV_RANK), lambda i: (0, 0))],
        out_specs=[pl.BlockSpec((bm, MLA_Q_RANK), lambda i: (i, 0)),
                   pl.BlockSpec((bm, MLA_KV_RANK), lambda i: (i, 0)),
                   pl.BlockSpec((bm, 128), lambda i: (i, 0))],
        out_shape=[jax.ShapeDtypeStruct((m, MLA_Q_RANK), BF16),
                   jax.ShapeDtypeStruct((m, MLA_KV_RANK), BF16),
                   jax.ShapeDtypeStruct((m, 128), F32)],
        compiler_params=_cparams(("parallel",)),
        name="lat_proj",
    )(h, w_lat, gq.reshape(1, -1), gkv.reshape(1, -1))


def _mla_q_kernel(a_ref, w_ref, g_ref, c_ref, sa_ref, sb_ref, o_ref, *, heads):
    z = _dot(a_ref[...], w_ref[...])
    g = g_ref[...]
    c, sa, sb = c_ref[...], sa_ref[...], sb_ref[...]
    for h in range(heads):
        zh = z[:, h * MLA_HEAD_PAD:(h + 1) * MLA_HEAD_PAD]
        ms = jnp.sum(zh * zh, axis=-1, keepdims=True) * (1.0 / MLA_QK)
        y = zh * lax.rsqrt(ms + NORM_EPS) * g
        o_ref[:, h * MLA_HEAD_PAD:h * MLA_HEAD_PAD + 128] = y[:, :128].astype(o_ref.dtype)
        yr = _apply_rope(y[:, 128:], c, sa, sb, MLA_ROPE // 2)
        o_ref[:, h * MLA_HEAD_PAD + 128:(h + 1) * MLA_HEAD_PAD] = yr.astype(o_ref.dtype)


def _mla_q_proj(cq, w_uq, g_pad, tabs, seq, bm=1024, heads=4):
    m, k = cq.shape
    n = w_uq.shape[1]
    bn = heads * MLA_HEAD_PAD
    nsb = seq // bm
    tab_spec = pl.BlockSpec((bm, 128), lambda i, j: (i % nsb, 0))
    return pl.pallas_call(
        functools.partial(_mla_q_kernel, heads=heads),
        grid=(m // bm, n // bn),
        in_specs=[pl.BlockSpec((bm, k), lambda i, j: (i, 0)),
                  pl.BlockSpec((k, bn), lambda i, j: (0, j)),
                  pl.BlockSpec((1, MLA_HEAD_PAD), lambda i, j: (0, 0)),
                  tab_spec, tab_spec, tab_spec],
        out_specs=pl.BlockSpec((bm, bn), lambda i, j: (i, j)),
        out_shape=jax.ShapeDtypeStruct((m, n), BF16),
        compiler_params=_cparams(("parallel", "parallel")),
        name="mla_q_proj",
    )(cq, w_uq, g_pad, *tabs)


def _mla_kv_kernel(a_ref, w_ref, kpe_ref, g_ref, c_ref, sa_ref, sb_ref, k_ref, v_ref, *, heads):
    z = _dot(a_ref[...], w_ref[...])
    kpe = kpe_ref[...]
    g = g_ref[...]
    g_nope, g_rope = g[:, :128], g[:, 128:]
    c, sa, sb = c_ref[...], sa_ref[...], sb_ref[...]
    ss_pe = jnp.sum(kpe * kpe, axis=-1, keepdims=True)
    one_col = (lax.broadcasted_iota(jnp.int32, (1, 128), 1) == 0).astype(v_ref.dtype)
    kpe_rot = _apply_rope(kpe * g_rope, c, sa, sb, MLA_ROPE // 2)
    for h in range(heads):
        nope = z[:, h * 256:h * 256 + 128]
        ms = (jnp.sum(nope * nope, axis=-1, keepdims=True) + ss_pe) * (1.0 / MLA_QK)
        r = lax.rsqrt(ms + NORM_EPS)
        k_ref[:, h * MLA_HEAD_PAD:h * MLA_HEAD_PAD + 128] = (nope * r * g_nope).astype(k_ref.dtype)
        k_ref[:, h * MLA_HEAD_PAD + 128:(h + 1) * MLA_HEAD_PAD] = (kpe_rot * r).astype(k_ref.dtype)
        v_ref[:, h * MLA_V_PAD:h * MLA_V_PAD + MLA_V] = z[:, h * 256 + 128:(h + 1) * 256].astype(v_ref.dtype)
        v_ref[:, h * MLA_V_PAD + MLA_V:(h + 1) * MLA_V_PAD] = jnp.broadcast_to(one_col, (z.shape[0], 128))


def _mla_kv_proj(ckv, w_ukv, kpe, g_pad, tabs, seq, bm=1024, heads=4):
    m, k = ckv.shape
    nsb = seq // bm
    tab_spec = pl.BlockSpec((bm, 128), lambda i, j: (i % nsb, 0))
    return pl.pallas_call(
        functools.partial(_mla_kv_kernel, heads=heads),
        grid=(m // bm, MLA_HEADS // heads),
        in_specs=[pl.BlockSpec((bm, k), lambda i, j: (i, 0)),
                  pl.BlockSpec((k, heads * 256), lambda i, j: (0, j)),
                  pl.BlockSpec((bm, 128), lambda i, j: (i, 0)),
                  pl.BlockSpec((1, MLA_HEAD_PAD), lambda i, j: (0, 0)),
                  tab_spec, tab_spec, tab_spec],
        out_specs=[pl.BlockSpec((bm, heads * MLA_HEAD_PAD), lambda i, j: (i, j)),
                   pl.BlockSpec((bm, heads * MLA_V_PAD), lambda i, j: (i, j))],
        out_shape=[jax.ShapeDtypeStruct((m, MLA_HEADS * MLA_HEAD_PAD), BF16),
                   jax.ShapeDtypeStruct((m, MLA_HEADS * MLA_V_PAD), BF16)],
        compiler_params=_cparams(("parallel", "parallel")),
        name="mla_kv_proj",
    )(ckv, w_ukv, kpe, g_pad, *tabs)


def _heads_kernel(a_ref, w_ref, g_ref, o_ref, *, heads, n_norm):
    z = _dot(a_ref[...], w_ref[...])
    j = pl.program_id(1)

    @pl.when(j < n_norm)
    def _():
        g = g_ref[0]
        for h in range(heads):
            zh = z[:, h * 128:(h + 1) * 128]
            y = zh * lax.rsqrt(jnp.mean(zh * zh, axis=-1, keepdims=True) + NORM_EPS) * g
            o_ref[:, h * 128:(h + 1) * 128] = y.astype(o_ref.dtype)

    @pl.when(j >= n_norm)
    def _():
        o_ref[...] = z.astype(o_ref.dtype)


def _heads_proj(a, w, gains, n_q_blocks, bm=1024, bn=1024, name="heads_proj"):
    m, k = a.shape
    n = w.shape[1]
    g3 = jnp.concatenate([gains, jnp.ones((1, 128), F32)], axis=0).reshape(3, 1, 128)
    return pl.pallas_call(
        functools.partial(_heads_kernel, heads=bn // 128, n_norm=2 * n_q_blocks),
        grid=(m // bm, n // bn),
        in_specs=[pl.BlockSpec((bm, k), lambda i, j: (i, 0)),
                  pl.BlockSpec((k, bn), lambda i, j: (0, j)),
                  pl.BlockSpec((1, 1, 128), lambda i, j: (j // n_q_blocks, 0, 0))],
        out_specs=pl.BlockSpec((bm, bn), lambda i, j: (i, j)),
        out_shape=jax.ShapeDtypeStruct((m, n), BF16),
        compiler_params=_cparams(("parallel", "parallel")),
        name=name,
    )(a, w, g3)


def _flash_stages(q_ref, k_ref, v_ref, o_ref, s_w, s_r, p_w, p_r, al_w, al_r, mr_w, mr_r, m_ref,
                  acc_ref, *, total, nk):
    t = pl.program_id(0)
    s = _dot_nt(q_ref[...], k_ref[...])
    s_w[...] = s
    mr_w[...] = jnp.max(s, axis=-1, keepdims=True)

    valid_b = jnp.logical_and(t >= 1, t <= total)
    ki_b = jnp.clip(t - 1, 0, total - 1) % nk
    m_old = m_ref[...]
    m_prev = jnp.where(ki_b == 0, -jnp.inf, m_old)
    m_new = jnp.maximum(m_prev, mr_r[...])
    al_w[...] = jnp.exp2(m_prev - m_new)
    p_w[...] = jnp.exp2(s_r[...] - m_new).astype(p_w.dtype)
    m_ref[...] = jnp.where(valid_b, m_new, m_old)

    acc = al_r[...] * acc_ref[...] + _dot(p_r[...], v_ref[...])
    acc_ref[...] = acc
    ki_c = jnp.clip(t - 2, 0, total - 1) % nk

    @pl.when(jnp.logical_and(t >= 2, ki_c == nk - 1))
    def _():
        o_ref[...] = (acc[:, :MLA_V] / acc[:, MLA_V:MLA_V + 1]).astype(o_ref.dtype)


def _flash_kernel(q_ref, k_ref, v_ref, o_ref, s0, s1, p0, p1, al0, al1, mr0, mr1, m_ref, acc_ref, **kw):
    t = pl.program_id(0)
    s_scr, p_scr, al_scr, mr_scr = (s0, s1), (p0, p1), (al0, al1), (mr0, mr1)

    @pl.when(t == 0)
    def _():
        for ref in (s0, s1, p0, p1, al0, al1, mr0, mr1, m_ref, acc_ref):
            ref[...] = jnp.zeros(ref.shape, ref.dtype)

    for par in (0, 1):
        @pl.when(t % 2 == par)
        def _(par=par):
            _flash_stages(q_ref, k_ref, v_ref, o_ref, s_scr[par], s_scr[1 - par],
                          p_scr[1 - par], p_scr[par], al_scr[1 - par], al_scr[par],
                          mr_scr[par], mr_scr[1 - par], m_ref, acc_ref, **kw)


def _mla_attention(q, k, v, batch, seq, tq=1024, tk=2048):
    m = q.shape[0]
    tk = min(tk, seq // 2)
    nq, nk = seq // tq, seq // tk
    total = batch * MLA_HEADS * nq * nk

    def unravel(tt):
        ki = tt % nk
        r = tt // nk
        qi = r % nq
        r = r // nq
        return r // MLA_HEADS, r % MLA_HEADS, qi, ki

    def q_map(t):
        b, h, qi, _ = unravel(jnp.minimum(t, total - 1))
        return (b * nq + qi, h)

    def k_map(t):
        b, h, _, ki = unravel(jnp.minimum(t, total - 1))
        return (b * nk + ki, h)

    def v_map(t):
        b, h, _, ki = unravel(jnp.clip(t - 2, 0, total - 1))
        return (b * nk + ki, h)

    def o_map(t):
        b, h, qi, _ = unravel(jnp.clip(t - 2, 0, total - 1))
        return (b * nq + qi, h)

    def col():
        return pltpu.VMEM((tq, 1), F32)

    return pl.pallas_call(
        functools.partial(_flash_kernel, total=total, nk=nk),
        grid=(total + 2,),
        in_specs=[pl.BlockSpec((tq, MLA_HEAD_PAD), q_map),
                  pl.BlockSpec((tk, MLA_HEAD_PAD), k_map),
                  pl.BlockSpec((tk, MLA_V_PAD), v_map)],
        out_specs=pl.BlockSpec((tq, MLA_V), o_map),
        out_shape=jax.ShapeDtypeStruct((m, MLA_HEADS * MLA_V), BF16),
        scratch_shapes=[pltpu.VMEM((tq, tk), F32), pltpu.VMEM((tq, tk), F32),
                        pltpu.VMEM((tq, tk), BF16), pltpu.VMEM((tq, tk), BF16),
                        col(), col(), col(), col(), col(), pltpu.VMEM((tq, MLA_V_PAD), F32)],
        compiler_params=_cparams(("arbitrary",)),
        name="mla_attention",
    )(q, k, v)


DIL_SQ = 256
DIL_SK = DIL_SQ + 2 * DIL_RADIUS


def _dil_kernel(q_ref, kp_ref, k_ref, kn_ref, vp_ref, v_ref, vn_ref, o_ref, st_ref,
                kwin_ref, vwin_ref, *, tq, length):
    r = DIL_RADIUS
    q0 = pl.program_id(2) * tq
    kwin_ref[0:r] = kp_ref[...]
    kwin_ref[r:r + tq] = k_ref[...]
    kwin_ref[r + tq:] = kn_ref[...]
    vwin_ref[0:r] = vp_ref[...]
    vwin_ref[r:r + tq] = v_ref[...]
    vwin_ref[r + tq:] = vn_ref[...]
    qi = lax.broadcasted_iota(jnp.int32, (DIL_SQ, DIL_SK), 0)
    kj = lax.broadcasted_iota(jnp.int32, (DIL_SQ, DIL_SK), 1)
    band = (kj >= qi) & (kj <= qi + 2 * r)
    lane = lax.broadcasted_iota(jnp.int32, (1, 128), 1)
    for c in range(tq // DIL_SQ):
        kpos = q0 + c * DIL_SQ - r + kj
        valid = band & (kpos >= 0) & (kpos < length)
        st = jnp.zeros((DIL_SQ, 128), F32)
        for h in range(DIL_HPG):
            hs = slice(h * 128, (h + 1) * 128)
            qh = q_ref[c * DIL_SQ:(c + 1) * DIL_SQ, hs]
            kh = kwin_ref[c * DIL_SQ:c * DIL_SQ + DIL_SK, hs]
            vh = vwin_ref[c * DIL_SQ:c * DIL_SQ + DIL_SK, hs]
            s = jnp.where(valid, _dot_nt(qh, kh), NEG_INF)
            mx = jnp.max(s, axis=-1, keepdims=True)
            p = jnp.exp2(s - mx)
            den = jnp.sum(p, axis=-1, keepdims=True)
            o = _dot(p.astype(vh.dtype), vh) / den
            o_ref[c * DIL_SQ:(c + 1) * DIL_SQ, hs] = o
            st = jnp.where(lane == h, mx, st)
            st = jnp.where(lane == DIL_HPG + h, den, st)
        st_ref[c * DIL_SQ:(c + 1) * DIL_SQ, :] = st


def _dil_proj_kernel(a_ref, w_ref, g_ref, c_ref, sa_ref, sb_ref, o_ref, y_scr, *, dil, heads):
    z = _dot(a_ref[...], w_ref[...])
    part = pl.program_id(1)
    rows = z.shape[0] // dil

    @pl.when(part < 2)
    def _():
        g = g_ref[0]
        for h in range(heads):
            zh = z[:, h * 128:(h + 1) * 128]
            y = zh * lax.rsqrt(jnp.mean(zh * zh, axis=-1, keepdims=True) + NORM_EPS) * g
            y_scr[h] = _apply_rope(y, c_ref[...], sa_ref[...], sb_ref[...], DIL_ROT // 2)

    @pl.when(part == 2)
    def _():
        for h in range(heads):
            y_scr[h] = z[:, h * 128:(h + 1) * 128]

    for h in range(heads):
        hs = slice(h * 128, (h + 1) * 128)
        if dil == 1:
            o_ref[0, :, hs] = y_scr[h].astype(o_ref.dtype)
        else:
            for r in range(dil):
                o_ref[r, :, hs] = y_scr[h, pl.ds(r, rows, stride=dil), :].astype(o_ref.dtype)


def _dil_proj(h, w_b, gains, tabs, batch, seq, group, dil, bm=1024):
    m, k = h.shape
    w = DIL_HPG * 128
    nsb = seq // bm
    g3 = jnp.concatenate([gains, jnp.ones((1, 128), F32)], axis=0).reshape(3, 1, 128)
    tab_spec = pl.BlockSpec((bm, 128), lambda i, p: (i % nsb, 0))
    return pl.pallas_call(
        functools.partial(_dil_proj_kernel, dil=dil, heads=DIL_HPG),
        grid=(m // bm, 3),
        in_specs=[pl.BlockSpec((bm, k), lambda i, p: (i, 0)),
                  pl.BlockSpec((k, w), lambda i, p: (0, p * 3 + group)),
                  pl.BlockSpec((1, 1, 128), lambda i, p: (p, 0, 0)),
                  tab_spec, tab_spec, tab_spec],
        out_specs=pl.BlockSpec((None, dil, bm // dil, w), lambda i, p: (i // nsb, 0, i % nsb, p)),
        out_shape=jax.ShapeDtypeStruct((batch, dil, seq // dil, 3 * w), BF16),
        scratch_shapes=[pltpu.VMEM((DIL_HPG, bm, 128), F32)],
        compiler_params=_cparams(("parallel", "arbitrary")),
        name=f"dil_proj_g{group}",
    )(h, w_b, g3, *tabs)


def _dilated_group(qkv, group):
    batch, dil, length, _ = qkv.shape
    tq = min(512, length)
    r = DIL_RADIUS
    nq = length // tq
    hb = tq // r
    nhb = length // r
    w = DIL_HPG * 128

    def main(part):
        return pl.BlockSpec((None, None, tq, w), lambda b, rr, qi: (b, rr, qi, part))

    def prev(part):
        return pl.BlockSpec((None, None, r, w),
                            lambda b, rr, qi: (b, rr, jnp.maximum(qi * hb - 1, 0), part))

    def nxt(part):
        return pl.BlockSpec((None, None, r, w),
                            lambda b, rr, qi: (b, rr, jnp.minimum((qi + 1) * hb, nhb - 1), part))

    return pl.pallas_call(
        functools.partial(_dil_kernel, tq=tq, length=length),
        grid=(batch, dil, nq),
        in_specs=[main(0), prev(1), main(1), nxt(1), prev(2), main(2), nxt(2)],
        out_specs=[pl.BlockSpec((None, None, tq, w), lambda b, rr, qi: (b, rr, qi, 0)),
                   pl.BlockSpec((None, None, tq, 128), lambda b, rr, qi: (b, rr, qi, 0))],
        out_shape=[jax.ShapeDtypeStruct((batch, dil, length, w), F32),
                   jax.ShapeDtypeStruct((batch, dil, length, 128), F32)],
        scratch_shapes=[pltpu.VMEM((tq + 2 * r, w), BF16), pltpu.VMEM((tq + 2 * r, w), BF16)],
        compiler_params=_cparams(("parallel", "parallel", "parallel")),
        name=f"dilated_g{group}",
    )(qkv, qkv, qkv, qkv, qkv, qkv, qkv)


def _dil_merge_kernel(o0_ref, o1_ref, o2_ref, s0_ref, s1_ref, s2_ref, out_ref, on_scr, sn_scr,
                      *, dils):
    bm = out_ref.shape[0]
    for gi, (o_ref, s_ref) in enumerate(((o0_ref, s0_ref), (o1_ref, s1_ref), (o2_ref, s2_ref))):
        dil = dils[gi]
        for r in range(dil):
            rows = pl.ds(r, bm // dil, stride=dil) if dil > 1 else slice(None)
            sn_scr[gi, rows, :] = s_ref[r]
            for h in range(DIL_HPG):
                on_scr[gi, h, rows, :] = o_ref[r, :, h * 128:(h + 1) * 128]
    s0, s1, s2 = sn_scr[0], sn_scr[1], sn_scr[2]
    for h in range(DIL_HPG):
        hs = slice(h * 128, (h + 1) * 128)
        m0, m1, m2 = s0[:, h:h + 1], s1[:, h:h + 1], s2[:, h:h + 1]
        l0, l1, l2 = (s0[:, DIL_HPG + h:DIL_HPG + h + 1], s1[:, DIL_HPG + h:DIL_HPG + h + 1],
                      s2[:, DIL_HPG + h:DIL_HPG + h + 1])
        mx = jnp.maximum(jnp.maximum(m0, m1), m2)
        w0 = jnp.exp2(m0 - mx) * l0
        w1 = jnp.exp2(m1 - mx) * l1
        w2 = jnp.exp2(m2 - mx) * l2
        num = w0 * on_scr[0, h] + w1 * on_scr[1, h] + w2 * on_scr[2, h]
        out_ref[:, hs] = (num / (w0 + w1 + w2)).astype(out_ref.dtype)


def _dil_merge(outs, stats, batch, seq, bm=512):
    w = DIL_HPG * 128
    nsb = seq // bm
    dils = tuple(o.shape[1] for o in outs)

    def spec(dil, width):
        return pl.BlockSpec((None, dil, bm // dil, width), lambda b, i: (b, 0, i, 0))

    return pl.pallas_call(
        functools.partial(_dil_merge_kernel, dils=dils),
        grid=(batch, nsb),
        in_specs=[spec(d, w) for d in dils] + [spec(d, 128) for d in dils],
        out_specs=pl.BlockSpec((bm, w), lambda b, i: (b * nsb + i, 0)),
        out_shape=jax.ShapeDtypeStruct((batch * seq, w), BF16),
        scratch_shapes=[pltpu.VMEM((3, DIL_HPG, bm, 128), F32), pltpu.VMEM((3, bm, 128), F32)],
        compiler_params=_cparams(("parallel", "parallel")),
        name="dilated_merge",
    )(*outs, *stats)


NA_TQ = NA_QROWS * GRID_W
NA_TK = NA_KROWS * GRID_W
NA_HPS = 16


def _na_bias_tables(rpb, rows):
    heads = rpb.shape[0]
    cols = np.arange(GRID_W)
    cs = np.clip(cols - NA_COLS // 2, 0, GRID_W - NA_COLS)
    kc = np.arange(GRID_W)
    col_ok = (kc[None, :] >= cs[:, None]) & (kc[None, :] < cs[:, None] + NA_COLS)
    col_idx = np.clip(kc[None, :] - cols[:, None] + (NA_COLS - 1), 0, 2 * NA_COLS - 2)
    toe = jnp.where(col_ok[None, None], (rpb * LOG2E)[:, :, col_idx], NEG_INF)
    offs, oks = [], []
    for i0, ks in ((0, 0), (NA_QROWS * 2, NA_QROWS), (rows - NA_QROWS, rows - NA_KROWS)):
        qi = i0 + np.arange(NA_QROWS)
        rs = np.clip(qi - NA_ROWS // 2, 0, rows - NA_ROWS)
        kr = ks + np.arange(NA_KROWS)
        oks.append((kr[None, :] >= rs[:, None]) & (kr[None, :] < rs[:, None] + NA_ROWS))
        offs.append(ks - qi + (NA_ROWS - 1))
    offs, oks = np.stack(offs), np.stack(oks)
    idx = offs[:, :, None] + np.arange(NA_KROWS)[None, None, :]
    assert idx[oks].min() >= 0 and idx[oks].max() <= 2 * NA_ROWS - 2
    wide = jnp.concatenate([toe, toe], axis=-1)

    def table_kernel(t_ref, o_ref):
        left = lax.broadcasted_iota(jnp.int32, (GRID_W, 2 * GRID_W), 1) < GRID_W
        masked = jnp.full((GRID_W, 2 * GRID_W), NEG_INF, F32)
        for kind in range(3):
            @pl.when(pl.program_id(1) == kind)
            def _(kind=kind):
                for q in range(NA_QROWS):
                    for pr in range(NA_KROWS // 2):
                        a, b = 2 * pr, 2 * pr + 1
                        ta = t_ref[int(idx[kind, q, a])] if oks[kind, q, a] else masked
                        tb = t_ref[int(idx[kind, q, b])] if oks[kind, q, b] else masked
                        o_ref[q * GRID_W:(q + 1) * GRID_W, pr * 2 * GRID_W:(pr + 1) * 2 * GRID_W] = (
                            jnp.where(left, ta, tb))

    return pl.pallas_call(
        table_kernel,
        grid=(heads, 3),
        in_specs=[pl.BlockSpec((None, 2 * NA_ROWS - 1, GRID_W, 2 * GRID_W), lambda h, kind: (h, 0, 0, 0))],
        out_specs=pl.BlockSpec((None, None, NA_TQ, NA_TK), lambda h, kind: (kind, h, 0, 0)),
        out_shape=jax.ShapeDtypeStruct((3, heads, NA_TQ, NA_TK), F32),
        compiler_params=_cparams(("parallel", "arbitrary")),
        name="na_bias_table",
    )(wide)


def _na_kernel(q_ref, k0_ref, k1_ref, k2_ref, v0_ref, v1_ref, v2_ref, b_ref, o_ref):
    for h in range(NA_HPS):
        hs = slice(h * 128, (h + 1) * 128)
        q = q_ref[0, :, hs]
        k = jnp.concatenate([k0_ref[0, :, hs], k1_ref[0, :, hs], k2_ref[0, :, hs]], axis=0)
        v = jnp.concatenate([v0_ref[0, :, hs], v1_ref[0, :, hs], v2_ref[0, :, hs]], axis=0)
        s = _dot_nt(q, k) + b_ref[0, h]
        mx = jnp.max(s, axis=-1, keepdims=True)
        p = jnp.exp2(s - mx)
        den = jnp.sum(p, axis=-1, keepdims=True)
        o = _dot(p.astype(v.dtype), v) / den
        o_ref[0, :, hs] = o.astype(o_ref.dtype)


def _na_attention(qkv, bias, batch, seq):
    rows = seq // GRID_W
    nblk = rows // NA_QROWS
    nkb = NA_KROWS // NA_QROWS
    ng = NA_HEADS // NA_HPS
    w = NA_HPS * 128
    qv = qkv.reshape(batch, seq, qkv.shape[1])

    def kv_spec(part, off):
        return pl.BlockSpec(
            (1, NA_TQ, w),
            lambda b, g, rb: (b, jnp.clip(rb - 1, 0, nblk - nkb) + off, part * ng + g))

    def bias_map(b, g, rb):
        cfg = jnp.where(rb == 0, 0, jnp.where(rb == nblk - 1, 2, 1))
        return (cfg, g, 0, 0)

    o = pl.pallas_call(
        _na_kernel,
        grid=(batch, ng, nblk),
        in_specs=[pl.BlockSpec((1, NA_TQ, w), lambda b, g, rb: (b, rb, g)),
                  kv_spec(1, 0), kv_spec(1, 1), kv_spec(1, 2),
                  kv_spec(2, 0), kv_spec(2, 1), kv_spec(2, 2),
                  pl.BlockSpec((1, NA_HPS, NA_TQ, NA_TK), bias_map)],
        out_specs=pl.BlockSpec((1, NA_TQ, w), lambda b, g, rb: (b, rb, g)),
        out_shape=jax.ShapeDtypeStruct((batch, seq, NA_HEADS * 128), BF16),
        compiler_params=_cparams(("parallel", "parallel", "arbitrary")),
        name="na_attention",
    )(qv, qv, qv, qv, qv, qv, qv, bias)
    return o.reshape(batch * seq, NA_HEADS * 128)


def _ffn_up_kernel(h_ref, w_ref, a_ref, ap_ref, an_ref, cw_ref, cb_ref, o_ref, *, bm, seq):
    i = pl.program_id(0)
    u = _dot(h_ref[...], w_ref[...])
    a = a_ref[...]
    first = (i * bm) % seq == 0
    last = ((i + 1) * bm) % seq == 0
    prev_row = jnp.where(first, 0.0, ap_ref[7:8, :])
    next_row = jnp.where(last, 0.0, an_ref[0:1, :])
    ridx = lax.broadcasted_iota(jnp.int32, (bm, 1), 0)
    a_prev = jnp.where(ridx == 0, prev_row, pltpu.roll(a, 1, 0))
    a_next = jnp.where(ridx == bm - 1, next_row, pltpu.roll(a, bm - 1, 0))
    cw = cw_ref[...]
    ac = a_prev * cw[0:1] + a * cw[1:2] + a_next * cw[2:3] + cb_ref[...]
    o_ref[...] = (ac * jax.nn.sigmoid(ac) * u).astype(o_ref.dtype)


def _ffn_up(h, w_up, a, conv_w, conv_b, seq, bm=1024, bn=512):
    m, k = h.shape
    n = w_up.shape[1]
    rb = bm // 8
    nrb = m // 8
    return pl.pallas_call(
        functools.partial(_ffn_up_kernel, bm=bm, seq=seq),
        grid=(m // bm, n // bn),
        in_specs=[pl.BlockSpec((bm, k), lambda i, j: (i, 0)),
                  pl.BlockSpec((k, bn), lambda i, j: (0, j)),
                  pl.BlockSpec((bm, bn), lambda i, j: (i, j)),
                  pl.BlockSpec((8, bn), lambda i, j: (jnp.maximum(i * rb - 1, 0), j)),
                  pl.BlockSpec((8, bn), lambda i, j: (jnp.minimum((i + 1) * rb, nrb - 1), j)),
                  pl.BlockSpec((3, bn), lambda i, j: (0, j)),
                  pl.BlockSpec((1, bn), lambda i, j: (0, j))],
        out_specs=pl.BlockSpec((bm, bn), lambda i, j: (i, j)),
        out_shape=jax.ShapeDtypeStruct((m, n), BF16),
        compiler_params=_cparams(("parallel", "parallel")),
        name="ffn_up",
    )(h, w_up, a, a, a, conv_w, conv_b)


def _conv_ffn(x2d, seq, norm_g, w_gate, w_up, conv_w, conv_b, w_down, layer):
    pad = D_FF_PAD - D_FF
    wg = _cast_pad(w_gate, layer, cols_out=D_FF_PAD)
    wu = _cast_pad(w_up, layer, cols_out=D_FF_PAD)
    wd = _cast_pad(w_down, layer, rows_out=D_FF_PAD)
    cw = jnp.pad(conv_w, ((0, 0), (0, pad)))
    cb = jnp.pad(conv_b, ((0, pad),)).reshape(1, D_FF_PAD)
    h = _rmsnorm(x2d, norm_g)
    a = _mm(h, wg, F32, name="ffn_gate")
    g = _ffn_up(h, wu, a, cw, cb, seq)
    return _mm_res_acc(g, wd, x2d, bk=D_FF_PAD // 4, name="ffn_down")


def _even_layer(x2d, batch, seq, e_norm, w_in, q_lora_g, kv_lora_g, w_uq, w_ukv, mla_qn, mla_kn,
                dil_qn, dil_kn, w_out, layer):
    c3 = MLA_Q_RANK + MLA_KV_RANK + MLA_ROPE
    w_lat = jnp.pad(w_in[:, :c3], ((0, 0), (0, LAT_PAD - c3))).astype(BF16)
    w_b = w_in[:, c3:].astype(BF16)
    hpad = MLA_HEAD_PAD - MLA_QK
    w_uq_p = jnp.pad(w_uq.reshape(MLA_Q_RANK, MLA_HEADS, MLA_QK), ((0, 0), (0, 0), (0, hpad)))
    w_uq_p = w_uq_p.reshape(MLA_Q_RANK, MLA_HEADS * MLA_HEAD_PAD).astype(BF16)
    gq_pad = jnp.pad(mla_qn * (MLA_QK ** -0.5 * LOG2E), (0, hpad)).reshape(1, MLA_HEAD_PAD)
    gk_pad = jnp.pad(mla_kn, (0, hpad)).reshape(1, MLA_HEAD_PAD)
    mla_tabs = _rope_tables(seq, MLA_ROPE, 128)
    dil_tabs = _rope_tables(seq, DIL_ROT, 128)

    h = _rmsnorm(x2d, e_norm)
    cq, ckv, kpe = _lat_proj(h, w_lat, q_lora_g, kv_lora_g)
    q = _mla_q_proj(cq, w_uq_p, gq_pad, mla_tabs, seq)
    k, v = _mla_kv_proj(ckv, w_ukv.astype(BF16), kpe, gk_pad, mla_tabs, seq)
    o_a = _mla_attention(q, k, v, batch, seq)

    outs, stats = [], []
    for g, (window, dil) in enumerate(DIL_CONFIGS):
        assert window // (2 * dil) == DIL_RADIUS
        qkv_g = _dil_proj(h, w_b, jnp.stack([dil_qn * (HEAD_DIM ** -0.5 * LOG2E), dil_kn]), dil_tabs,
                          batch, seq, g, dil)
        o, st = _dilated_group(qkv_g, g)
        outs.append(o)
        stats.append(st)
    o_b = _dil_merge(outs, stats, batch, seq)
    return _mm_res2(o_a, o_b, _cast_pad(w_out, layer), x2d, name="even_out_proj")


def _odd_layer(x2d, batch, seq, o_norm, w_qkv, qn, kn, rpb, w_out, layer):
    h = _rmsnorm(x2d, o_norm)
    qkv = _heads_proj(h, _cast_pad(w_qkv, layer),
                      jnp.stack([qn * (HEAD_DIM ** -0.5 * LOG2E), kn]), NA_HEADS * 128 // 1024,
                      name="na_qkv_proj")
    bias = _na_bias_tables(rpb, seq // GRID_W)
    o = _na_attention(qkv, bias, batch, seq)
    return _mm_res(o, _cast_pad(w_out, layer), x2d, name="odd_out_proj")


def kernel(x, e_norm, e_w_in, e_q_lora_norm, e_kv_lora_norm, e_w_uq, e_w_ukv, e_mla_q_norm,
           e_mla_k_norm, e_dil_q_norm, e_dil_k_norm, e_w_out, o_norm, o_w_qkv, o_q_norm, o_k_norm,
           o_rpb, o_w_out, f_norm, f_w_gate, f_w_up, f_conv_w, f_conv_b, f_w_down):
    batch, seq, d = x.shape
    x2d = x.reshape(batch * seq, d)
    depth = f_norm.shape[0]
    for layer in range(depth):
        i = layer // 2
        if layer % 2 == 0:
            x2d = _even_layer(x2d, batch, seq, e_norm[i], e_w_in[i], e_q_lora_norm[i],
                              e_kv_lora_norm[i], e_w_uq[i], e_w_ukv[i], e_mla_q_norm[i],
                              e_mla_k_norm[i], e_dil_q_norm[i], e_dil_k_norm[i], e_w_out, i)
        else:
            x2d = _odd_layer(x2d, batch, seq, o_norm[i], o_w_qkv, o_q_norm[i], o_k_norm[i],
                             o_rpb[i], o_w_out, i)
        x2d = _conv_ffn(x2d, seq, f_norm[layer], f_w_gate, f_w_up, f_conv_w[layer],
                        f_conv_b[layer], f_w_down, layer)
    return x2d.reshape(batch, seq, d)
```

```python
import functools

import jax
import jax.numpy as jnp
import numpy as np
from jax import lax
from jax.experimental import pallas as pl
from jax.experimental.pallas import tpu as pltpu

F32 = jnp.float32
BF16 = jnp.bfloat16

HEAD_DIM = 128
ROPE_THETA = 500000.0
NORM_EPS = 1e-6
NEG_INF = -1e30

MLA_HEADS = 16
MLA_Q_RANK = 896
MLA_KV_RANK = 512
MLA_NOPE = 128
MLA_ROPE = 64
MLA_QK = MLA_NOPE + MLA_ROPE
MLA_V = 128
MLA_HEAD_PAD = 256
MLA_V_PAD = 256
LOG2E = float(np.log2(np.e))
LAT_PAD = 1536

DIL_CONFIGS = ((128, 1), (512, 4), (2048, 16))
DIL_HPG = 8
DIL_ROT = HEAD_DIM // 4
DIL_RADIUS = 64

GRID_W = 64
NA_HEADS = 32
NA_ROWS = 8
NA_COLS = 16
NA_QROWS = 4
NA_KROWS = 12

D_FF = 11008
D_FF_PAD = 11264

VMEM_LIMIT_BYTES = 56 * 1024 * 1024


def _cparams(sem):
    return pltpu.CompilerParams(dimension_semantics=sem, vmem_limit_bytes=VMEM_LIMIT_BYTES)


def _dot(a, b):
    return jnp.dot(a, b, preferred_element_type=F32)


def _dot_nt(a, b):
    return lax.dot_general(a, b, (((1,), (1,)), ((), ())), preferred_element_type=F32)


def _rmsnorm_kernel(x_ref, g_ref, o_ref):
    x = x_ref[...]
    y = x * lax.rsqrt(jnp.mean(x * x, axis=-1, keepdims=True) + NORM_EPS)
    o_ref[...] = (y * g_ref[...]).astype(o_ref.dtype)


def _rmsnorm(x2d, g, bm=512):
    m, d = x2d.shape
    return pl.pallas_call(
        _rmsnorm_kernel,
        grid=(m // bm,),
        in_specs=[pl.BlockSpec((bm, d), lambda i: (i, 0)),
                  pl.BlockSpec((1, d), lambda i: (0, 0))],
        out_specs=pl.BlockSpec((bm, d), lambda i: (i, 0)),
        out_shape=jax.ShapeDtypeStruct((m, d), BF16),
        compiler_params=_cparams(("parallel",)),
        name="rmsnorm",
    )(x2d, g.reshape(1, d))


def _cast_kernel(w_ref, o_ref, *, n_in_blocks, cols):
    i = pl.program_id(0)

    @pl.when(i < n_in_blocks)
    def _():
        o_ref[:, :cols] = w_ref[...].astype(o_ref.dtype)
        if o_ref.shape[1] > cols:
            o_ref[:, cols:] = jnp.zeros((o_ref.shape[0], o_ref.shape[1] - cols), o_ref.dtype)

    @pl.when(i >= n_in_blocks)
    def _():
        o_ref[...] = jnp.zeros(o_ref.shape, o_ref.dtype)


def _cast_pad(w_stack, layer, rows_out=None, cols_out=None, br=256):
    _, rows, cols = w_stack.shape
    rows_out = rows_out or rows
    cols_out = cols_out or cols
    nin = rows // br
    return pl.pallas_call(
        functools.partial(_cast_kernel, n_in_blocks=nin, cols=cols),
        grid=(rows_out // br,),
        in_specs=[pl.BlockSpec((None, br, cols), lambda i: (layer, jnp.minimum(i, nin - 1), 0))],
        out_specs=pl.BlockSpec((br, cols_out), lambda i: (i, 0)),
        out_shape=jax.ShapeDtypeStruct((rows_out, cols_out), BF16),
        compiler_params=_cparams(("parallel",)),
        name="cast_pad",
    )(w_stack)


def _mm_kernel(a_ref, w_ref, o_ref):
    o_ref[...] = _dot(a_ref[...], w_ref[...]).astype(o_ref.dtype)


def _mm(a, w, out_dtype, bm=1024, bn=1024, name="mm"):
    m, k = a.shape
    n = w.shape[1]
    return pl.pallas_call(
        _mm_kernel,
        grid=(m // bm, n // bn),
        in_specs=[pl.BlockSpec((bm, k), lambda i, j: (i, 0)),
                  pl.BlockSpec((k, bn), lambda i, j: (0, j))],
        out_specs=pl.BlockSpec((bm, bn), lambda i, j: (i, j)),
        out_shape=jax.ShapeDtypeStruct((m, n), out_dtype),
        compiler_params=_cparams(("parallel", "parallel")),
        name=name,
    )(a, w)


def _mm_res_kernel(a_ref, w_ref, r_ref, o_ref):
    o_ref[...] = r_ref[...] + _dot(a_ref[...], w_ref[...])


def _mm_res(a, w, res, bm=1024, bn=512, name="mm_res"):
    m, k = a.shape
    n = w.shape[1]
    return pl.pallas_call(
        _mm_res_kernel,
        grid=(m // bm, n // bn),
        in_specs=[pl.BlockSpec((bm, k), lambda i, j: (i, 0)),
                  pl.BlockSpec((k, bn), lambda i, j: (0, j)),
                  pl.BlockSpec((bm, bn), lambda i, j: (i, j))],
        out_specs=pl.BlockSpec((bm, bn), lambda i, j: (i, j)),
        out_shape=jax.ShapeDtypeStruct((m, n), F32),
        compiler_params=_cparams(("parallel", "parallel")),
        name=name,
    )(a, w, res)


def _mm_res2_kernel(a1_ref, a2_ref, w1_ref, w2_ref, r_ref, o_ref):
    o_ref[...] = r_ref[...] + _dot(a1_ref[...], w1_ref[...]) + _dot(a2_ref[...], w2_ref[...])


def _mm_res2(a1, a2, w, res, bm=1024, bn=512, name="mm_res2"):
    m, k1 = a1.shape
    k2 = a2.shape[1]
    n = w.shape[1]
    assert k1 % k2 == 0 and w.shape[0] == k1 + k2
    return pl.pallas_call(
        _mm_res2_kernel,
        grid=(m // bm, n // bn),
        in_specs=[pl.BlockSpec((bm, k1), lambda i, j: (i, 0)),
                  pl.BlockSpec((bm, k2), lambda i, j: (i, 0)),
                  pl.BlockSpec((k1, bn), lambda i, j: (0, j)),
                  pl.BlockSpec((k2, bn), lambda i, j: (k1 // k2, j)),
                  pl.BlockSpec((bm, bn), lambda i, j: (i, j))],
        out_specs=pl.BlockSpec((bm, bn), lambda i, j: (i, j)),
        out_shape=jax.ShapeDtypeStruct((m, n), F32),
        compiler_params=_cparams(("parallel", "parallel")),
        name=name,
    )(a1, a2, w, w, res)


def _mm_res_acc_kernel(a_ref, w_ref, r_ref, o_ref, acc_ref):
    k = pl.program_id(2)

    @pl.when(k == 0)
    def _():
        acc_ref[...] = jnp.zeros(acc_ref.shape, F32)

    acc_ref[...] += _dot(a_ref[...], w_ref[...])

    @pl.when(k == pl.num_programs(2) - 1)
    def _():
        o_ref[...] = r_ref[...] + acc_ref[...]


def _mm_res_acc(a, w, res, bk, bm=1024, bn=1024, name="mm_res_acc"):
    m, k = a.shape
    n = w.shape[1]
    return pl.pallas_call(
        _mm_res_acc_kernel,
        grid=(m // bm, n // bn, k // bk),
        in_specs=[pl.BlockSpec((bm, bk), lambda i, j, kk: (i, kk)),
                  pl.BlockSpec((bk, bn), lambda i, j, kk: (kk, j)),
                  pl.BlockSpec((bm, bn), lambda i, j, kk: (i, j))],
        out_specs=pl.BlockSpec((bm, bn), lambda i, j, kk: (i, j)),
        out_shape=jax.ShapeDtypeStruct((m, n), F32),
        scratch_shapes=[pltpu.VMEM((bm, bn), F32)],
        compiler_params=_cparams(("parallel", "parallel", "arbitrary")),
        name=name,
    )(a, w, res)


def _rope_tables(seq, rot, width):
    half = rot // 2
    inv_freq = 1.0 / (ROPE_THETA ** (jnp.arange(half, dtype=F32) * (2.0 / rot)))
    ang = jnp.arange(seq, dtype=jnp.int32).astype(F32)[:, None] * inv_freq[None, :]
    cos, sin = jnp.cos(ang), jnp.sin(ang)
    zh = jnp.zeros((seq, half), F32)
    rest = width - rot
    c = jnp.concatenate([cos, cos, jnp.ones((seq, rest), F32)], axis=1)
    sa = jnp.concatenate([zh, sin, jnp.zeros((seq, rest), F32)], axis=1)
    sb = jnp.concatenate([-sin, zh, jnp.zeros((seq, rest), F32)], axis=1)
    return c, sa, sb


def _apply_rope(y, c, sa, sb, half):
    width = y.shape[-1]
    return y * c + pltpu.roll(y, half, 1) * sa + pltpu.roll(y, width - half, 1) * sb


def _lat_kernel(a_ref, w_ref, gq_ref, gkv_ref, cq_ref, ckv_ref, kpe_ref):
    z = _dot(a_ref[...], w_ref[...])
    c1 = MLA_Q_RANK
    c2 = c1 + MLA_KV_RANK
    cq = z[:, :c1]
    ckv = z[:, c1:c2]
    cq = cq * lax.rsqrt(jnp.mean(cq * cq, axis=-1, keepdims=True) + NORM_EPS)
    ckv = ckv * lax.rsqrt(jnp.mean(ckv * ckv, axis=-1, keepdims=True) + NORM_EPS)
    cq_ref[...] = (cq * gq_ref[...]).astype(cq_ref.dtype)
    ckv_ref[...] = (ckv * gkv_ref[...]).astype(ckv_ref.dtype)
    kpe_ref[...] = z[:, c2:]


def _lat_proj(h, w_lat, gq, gkv, bm=512):
    m, k = h.shape
    return pl.pallas_call(
        _lat_kernel,
        grid=(m // bm,),
        in_specs=[pl.BlockSpec((bm, k), lambda i: (i, 0)),
                  pl.BlockSpec((k, LAT_PAD), lambda i: (0, 0)),
                  pl.BlockSpec((1, MLA_Q_RANK), lambda i: (0, 0)),
                  pl.BlockSpec((1, MLA_KV_RANK), lambda i: (0, 0))],
        out_specs=[pl.BlockSpec((bm, MLA_Q_RANK), lambda i: (i, 0)),
                   pl.BlockSpec((bm, MLA_KV_RANK), lambda i: (i, 0)),
                   pl.BlockSpec((bm, 128), lambda i: (i, 0))],
        out_shape=[jax.ShapeDtypeStruct((m, MLA_Q_RANK), BF16),
                   jax.ShapeDtypeStruct((m, MLA_KV_RANK), BF16),
                   jax.ShapeDtypeStruct((m, 128), F32)],
        compiler_params=_cparams(("parallel",)),
        name="lat_proj",
    )(h, w_lat, gq.reshape(1, -1), gkv.reshape(1, -1))


def _mla_q_kernel(a_ref, w_ref, g_ref, c_ref, sa_ref, sb_ref, o_ref, *, heads):
    z = _dot(a_ref[...], w_ref[...])
    g = g_ref[...]
    c, sa, sb = c_ref[...], sa_ref[...], sb_ref[...]
    for h in range(heads):
        zh = z[:, h * MLA_HEAD_PAD:(h + 1) * MLA_HEAD_PAD]
        ms = jnp.sum(zh * zh, axis=-1, keepdims=True) * (1.0 / MLA_QK)
        y = zh * lax.rsqrt(ms + NORM_EPS) * g
        o_ref[:, h * MLA_HEAD_PAD:h * MLA_HEAD_PAD + 128] = y[:, :128].astype(o_ref.dtype)
        yr = _apply_rope(y[:, 128:], c, sa, sb, MLA_ROPE // 2)
        o_ref[:, h * MLA_HEAD_PAD + 128:(h + 1) * MLA_HEAD_PAD] = yr.astype(o_ref.dtype)


def _mla_q_proj(cq, w_uq, g_pad, tabs, seq, bm=1024, heads=4):
    m, k = cq.shape
    n = w_uq.shape[1]
    bn = heads * MLA_HEAD_PAD
    nsb = seq // bm
    tab_spec = pl.BlockSpec((bm, 128), lambda i, j: (i % nsb, 0))
    return pl.pallas_call(
        functools.partial(_mla_q_kernel, heads=heads),
        grid=(m // bm, n // bn),
        in_specs=[pl.BlockSpec((bm, k), lambda i, j: (i, 0)),
                  pl.BlockSpec((k, bn), lambda i, j: (0, j)),
                  pl.BlockSpec((1, MLA_HEAD_PAD), lambda i, j: (0, 0)),
                  tab_spec, tab_spec, tab_spec],
        out_specs=pl.BlockSpec((bm, bn), lambda i, j: (i, j)),
        out_shape=jax.ShapeDtypeStruct((m, n), BF16),
        compiler_params=_cparams(("parallel", "parallel")),
        name="mla_q_proj",
    )(cq, w_uq, g_pad, *tabs)


def _mla_kv_kernel(a_ref, w_ref, kpe_ref, g_ref, c_ref, sa_ref, sb_ref, k_ref, v_ref, *, heads):
    z = _dot(a_ref[...], w_ref[...])
    kpe = kpe_ref[...]
    g = g_ref[...]
    g_nope, g_rope = g[:, :128], g[:, 128:]
    c, sa, sb = c_ref[...], sa_ref[...], sb_ref[...]
    ss_pe = jnp.sum(kpe * kpe, axis=-1, keepdims=True)
    one_col = (lax.broadcasted_iota(jnp.int32, (1, 128), 1) == 0).astype(v_ref.dtype)
    kpe_rot = _apply_rope(kpe * g_rope, c, sa, sb, MLA_ROPE // 2)
    for h in range(heads):
        nope = z[:, h * 256:h * 256 + 128]
        ms = (jnp.sum(nope * nope, axis=-1, keepdims=True) + ss_pe) * (1.0 / MLA_QK)
        r = lax.rsqrt(ms + NORM_EPS)
        k_ref[:, h * MLA_HEAD_PAD:h * MLA_HEAD_PAD + 128] = (nope * r * g_nope).astype(k_ref.dtype)
        k_ref[:, h * MLA_HEAD_PAD + 128:(h + 1) * MLA_HEAD_PAD] = (kpe_rot * r).astype(k_ref.dtype)
        v_ref[:, h * MLA_V_PAD:h * MLA_V_PAD + MLA_V] = z[:, h * 256 + 128:(h + 1) * 256].astype(v_ref.dtype)
        v_ref[:, h * MLA_V_PAD + MLA_V:(h + 1) * MLA_V_PAD] = jnp.broadcast_to(one_col, (z.shape[0], 128))


def _mla_kv_proj(ckv, w_ukv, kpe, g_pad, tabs, seq, bm=1024, heads=4):
    m, k = ckv.shape
    nsb = seq // bm
    tab_spec = pl.BlockSpec((bm, 128), lambda i, j: (i % nsb, 0))
    return pl.pallas_call(
        functools.partial(_mla_kv_kernel, heads=heads),
        grid=(m // bm, MLA_HEADS // heads),
        in_specs=[pl.BlockSpec((bm, k), lambda i, j: (i, 0)),
                  pl.BlockSpec((k, heads * 256), lambda i, j: (0, j)),
                  pl.BlockSpec((bm, 128), lambda i, j: (i, 0)),
                  pl.BlockSpec((1, MLA_HEAD_PAD), lambda i, j: (0, 0)),
                  tab_spec, tab_spec, tab_spec],
        out_specs=[pl.BlockSpec((bm, heads * MLA_HEAD_PAD), lambda i, j: (i, j)),
                   pl.BlockSpec((bm, heads * MLA_V_PAD), lambda i, j: (i, j))],
        out_shape=[jax.ShapeDtypeStruct((m, MLA_HEADS * MLA_HEAD_PAD), BF16),
                   jax.ShapeDtypeStruct((m, MLA_HEADS * MLA_V_PAD), BF16)],
        compiler_params=_cparams(("parallel", "parallel")),
        name="mla_kv_proj",
    )(ckv, w_ukv, kpe, g_pad, *tabs)


def _heads_kernel(a_ref, w_ref, g_ref, o_ref, *, heads, n_norm):
    z = _dot(a_ref[...], w_ref[...])
    j = pl.program_id(1)

    @pl.when(j < n_norm)
    def _():
        g = g_ref[0]
        for h in range(heads):
            zh = z[:, h * 128:(h + 1) * 128]
            y = zh * lax.rsqrt(jnp.mean(zh * zh, axis=-1, keepdims=True) + NORM_EPS) * g
            o_ref[:, h * 128:(h + 1) * 128] = y.astype(o_ref.dtype)

    @pl.when(j >= n_norm)
    def _():
        o_ref[...] = z.astype(o_ref.dtype)


def _heads_proj(a, w, gains, n_q_blocks, bm=1024, bn=1024, name="heads_proj"):
    m, k = a.shape
    n = w.shape[1]
    g3 = jnp.concatenate([gains, jnp.ones((1, 128), F32)], axis=0).reshape(3, 1, 128)
    return pl.pallas_call(
        functools.partial(_heads_kernel, heads=bn // 128, n_norm=2 * n_q_blocks),
        grid=(m // bm, n // bn),
        in_specs=[pl.BlockSpec((bm, k), lambda i, j: (i, 0)),
                  pl.BlockSpec((k, bn), lambda i, j: (0, j)),
                  pl.BlockSpec((1, 1, 128), lambda i, j: (j // n_q_blocks, 0, 0))],
        out_specs=pl.BlockSpec((bm, bn), lambda i, j: (i, j)),
        out_shape=jax.ShapeDtypeStruct((m, n), BF16),
        compiler_params=_cparams(("parallel", "parallel")),
        name=name,
    )(a, w, g3)


def _flash_stages(q_ref, k_ref, v_ref, o_ref, s_w, s_r, p_w, p_r, al_w, al_r, mr_w, mr_r, m_ref,
                  acc_ref, *, total, nk):
    t = pl.program_id(0)
    s = _dot_nt(q_ref[...], k_ref[...])
    s_w[...] = s
    mr_w[...] = jnp.max(s, axis=-1, keepdims=True)

    valid_b = jnp.logical_and(t >= 1, t <= total)
    ki_b = jnp.clip(t - 1, 0, total - 1) % nk
    m_old = m_ref[...]
    m_prev = jnp.where(ki_b == 0, -jnp.inf, m_old)
    m_new = jnp.maximum(m_prev, mr_r[...])
    al_w[...] = jnp.exp2(m_prev - m_new)
    p_w[...] = jnp.exp2(s_r[...] - m_new).astype(p_w.dtype)
    m_ref[...] = jnp.where(valid_b, m_new, m_old)

    acc = al_r[...] * acc_ref[...] + _dot(p_r[...], v_ref[...])
    acc_ref[...] = acc
    ki_c = jnp.clip(t - 2, 0, total - 1) % nk

    @pl.when(jnp.logical_and(t >= 2, ki_c == nk - 1))
    def _():
        o_ref[...] = (acc[:, :MLA_V] / acc[:, MLA_V:MLA_V + 1]).astype(o_ref.dtype)


def _flash_kernel(q_ref, k_ref, v_ref, o_ref, s0, s1, p0, p1, al0, al1, mr0, mr1, m_ref, acc_ref, **kw):
    t = pl.program_id(0)
    s_scr, p_scr, al_scr, mr_scr = (s0, s1), (p0, p1), (al0, al1), (mr0, mr1)

    @pl.when(t == 0)
    def _():
        for ref in (s0, s1, p0, p1, al0, al1, mr0, mr1, m_ref, acc_ref):
            ref[...] = jnp.zeros(ref.shape, ref.dtype)

    for par in (0, 1):
        @pl.when(t % 2 == par)
        def _(par=par):
            _flash_stages(q_ref, k_ref, v_ref, o_ref, s_scr[par], s_scr[1 - par],
                          p_scr[1 - par], p_scr[par], al_scr[1 - par], al_scr[par],
                          mr_scr[par], mr_scr[1 - par], m_ref, acc_ref, **kw)


def _mla_attention(q, k, v, batch, seq, tq=1024, tk=2048):
    m = q.shape[0]
    tk = min(tk, seq // 2)
    nq, nk = seq // tq, seq // tk
    total = batch * MLA_HEADS * nq * nk

    def unravel(tt):
        ki = tt % nk
        r = tt // nk
        qi = r % nq
        r = r // nq
        return r // MLA_HEADS, r % MLA_HEADS, qi, ki

    def q_map(t):
        b, h, qi, _ = unravel(jnp.minimum(t, total - 1))
        return (b * nq + qi, h)

    def k_map(t):
        b, h, _, ki = unravel(jnp.minimum(t, total - 1))
        return (b * nk + ki, h)

    def v_map(t):
        b, h, _, ki = unravel(jnp.clip(t - 2, 0, total - 1))
        return (b * nk + ki, h)

    def o_map(t):
        b, h, qi, _ = unravel(jnp.clip(t - 2, 0, total - 1))
        return (b * nq + qi, h)

    def col():
        return pltpu.VMEM((tq, 1), F32)

    return pl.pallas_call(
        functools.partial(_flash_kernel, total=total, nk=nk),
        grid=(total + 2,),
        in_specs=[pl.BlockSpec((tq, MLA_HEAD_PAD), q_map),
                  pl.BlockSpec((tk, MLA_HEAD_PAD), k_map),
                  pl.BlockSpec((tk, MLA_V_PAD), v_map)],
        out_specs=pl.BlockSpec((tq, MLA_V), o_map),
        out_shape=jax.ShapeDtypeStruct((m, MLA_HEADS * MLA_V), BF16),
        scratch_shapes=[pltpu.VMEM((tq, tk), F32), pltpu.VMEM((tq, tk), F32),
                        pltpu.VMEM((tq, tk), BF16), pltpu.VMEM((tq, tk), BF16),
                        col(), col(), col(), col(), col(), pltpu.VMEM((tq, MLA_V_PAD), F32)],
        compiler_params=_cparams(("arbitrary",)),
        name="mla_attention",
    )(q, k, v)


DIL_SQ = 256
DIL_SK = DIL_SQ + 2 * DIL_RADIUS


def _dil_kernel(q_ref, kp_ref, k_ref, kn_ref, vp_ref, v_ref, vn_ref, o_ref, st_ref,
                kwin_ref, vwin_ref, *, tq, length):
    r = DIL_RADIUS
    q0 = pl.program_id(2) * tq
    kwin_ref[0:r] = kp_ref[...]
    kwin_ref[r:r + tq] = k_ref[...]
    kwin_ref[r + tq:] = kn_ref[...]
    vwin_ref[0:r] = vp_ref[...]
    vwin_ref[r:r + tq] = v_ref[...]
    vwin_ref[r + tq:] = vn_ref[...]
    qi = lax.broadcasted_iota(jnp.int32, (DIL_SQ, DIL_SK), 0)
    kj = lax.broadcasted_iota(jnp.int32, (DIL_SQ, DIL_SK), 1)
    band = (kj >= qi) & (kj <= qi + 2 * r)
    lane = lax.broadcasted_iota(jnp.int32, (1, 128), 1)
    for c in range(tq // DIL_SQ):
        kpos = q0 + c * DIL_SQ - r + kj
        valid = band & (kpos >= 0) & (kpos < length)
        st = jnp.zeros((DIL_SQ, 128), F32)
        for h in range(DIL_HPG):
            hs = slice(h * 128, (h + 1) * 128)
            qh = q_ref[c * DIL_SQ:(c + 1) * DIL_SQ, hs]
            kh = kwin_ref[c * DIL_SQ:c * DIL_SQ + DIL_SK, hs]
            vh = vwin_ref[c * DIL_SQ:c * DIL_SQ + DIL_SK, hs]
            s = jnp.where(valid, _dot_nt(qh, kh), NEG_INF)
            mx = jnp.max(s, axis=-1, keepdims=True)
            p = jnp.exp2(s - mx)
            den = jnp.sum(p, axis=-1, keepdims=True)
            o = _dot(p.astype(vh.dtype), vh) / den
            o_ref[c * DIL_SQ:(c + 1) * DIL_SQ, hs] = o
            st = jnp.where(lane == h, mx, st)
            st = jnp.where(lane == DIL_HPG + h, den, st)
        st_ref[c * DIL_SQ:(c + 1) * DIL_SQ, :] = st


def _dil_proj_kernel(a_ref, w_ref, g_ref, c_ref, sa_ref, sb_ref, o_ref, y_scr, *, dil, heads):
    z = _dot(a_ref[...], w_ref[...])
    part = pl.program_id(1)
    rows = z.shape[0] // dil

    @pl.when(part < 2)
    def _():
        g = g_ref[0]
        for h in range(heads):
            zh = z[:, h * 128:(h + 1) * 128]
            y = zh * lax.rsqrt(jnp.mean(zh * zh, axis=-1, keepdims=True) + NORM_EPS) * g
            y_scr[h] = _apply_rope(y, c_ref[...], sa_ref[...], sb_ref[...], DIL_ROT // 2)

    @pl.when(part == 2)
    def _():
        for h in range(heads):
            y_scr[h] = z[:, h * 128:(h + 1) * 128]

    for h in range(heads):
        hs = slice(h * 128, (h + 1) * 128)
        if dil == 1:
            o_ref[0, :, hs] = y_scr[h].astype(o_ref.dtype)
        else:
            for r in range(dil):
                o_ref[r, :, hs] = y_scr[h, pl.ds(r, rows, stride=dil), :].astype(o_ref.dtype)


def _dil_proj(h, w_b, gains, tabs, batch, seq, group, dil, bm=1024):
    m, k = h.shape
    w = DIL_HPG * 128
    nsb = seq // bm
    g3 = jnp.concatenate([gains, jnp.ones((1, 128), F32)], axis=0).reshape(3, 1, 128)
    tab_spec = pl.BlockSpec((bm, 128), lambda i, p: (i % nsb, 0))
    return pl.pallas_call(
        functools.partial(_dil_proj_kernel, dil=dil, heads=DIL_HPG),
        grid=(m // bm, 3),
        in_specs=[pl.BlockSpec((bm, k), lambda i, p: (i, 0)),
                  pl.BlockSpec((k, w), lambda i, p: (0, p * 3 + group)),
                  pl.BlockSpec((1, 1, 128), lambda i, p: (p, 0, 0)),
                  tab_spec, tab_spec, tab_spec],
        out_specs=pl.BlockSpec((None, dil, bm // dil, w), lambda i, p: (i // nsb, 0, i % nsb, p)),
        out_shape=jax.ShapeDtypeStruct((batch, dil, seq // dil, 3 * w), BF16),
        scratch_shapes=[pltpu.VMEM((DIL_HPG, bm, 128), F32)],
        compiler_params=_cparams(("parallel", "arbitrary")),
        name=f"dil_proj_g{group}",
    )(h, w_b, g3, *tabs)


def _dilated_group(qkv, group):
    batch, dil, length, _ = qkv.shape
    tq = min(512, length)
    r = DIL_RADIUS
    nq = length // tq
    hb = tq // r
    nhb = length // r
    w = DIL_HPG * 128

    def main(part):
        return pl.BlockSpec((None, None, tq, w), lambda b, rr, qi: (b, rr, qi, part))

    def prev(part):
        return pl.BlockSpec((None, None, r, w),
                            lambda b, rr, qi: (b, rr, jnp.maximum(qi * hb - 1, 0), part))

    def nxt(part):
        return pl.BlockSpec((None, None, r, w),
                            lambda b, rr, qi: (b, rr, jnp.minimum((qi + 1) * hb, nhb - 1), part))

    return pl.pallas_call(
        functools.partial(_dil_kernel, tq=tq, length=length),
        grid=(batch, dil, nq),
        in_specs=[main(0), prev(1), main(1), nxt(1), prev(2), main(2), nxt(2)],
        out_specs=[pl.BlockSpec((None, None, tq, w), lambda b, rr, qi: (b, rr, qi, 0)),
                   pl.BlockSpec((None, None, tq, 128), lambda b, rr, qi: (b, rr, qi, 0))],
        out_shape=[jax.ShapeDtypeStruct((batch, dil, length, w), F32),
                   jax.ShapeDtypeStruct((batch, dil, length, 128), F32)],
        scratch_shapes=[pltpu.VMEM((tq + 2 * r, w), BF16), pltpu.VMEM((tq + 2 * r, w), BF16)],
        compiler_params=_cparams(("parallel", "parallel", "parallel")),
        name=f"dilated_g{group}",
    )(qkv, qkv, qkv, qkv, qkv, qkv, qkv)


def _dil_merge_kernel(o0_ref, o1_ref, o2_ref, s0_ref, s1_ref, s2_ref, out_ref, on_scr, sn_scr,
                      *, dils):
    bm = out_ref.shape[0]
    for gi, (o_ref, s_ref) in enumerate(((o0_ref, s0_ref), (o1_ref, s1_ref), (o2_ref, s2_ref))):
        dil = dils[gi]
        for r in range(dil):
            rows = pl.ds(r, bm // dil, stride=dil) if dil > 1 else slice(None)
            sn_scr[gi, rows, :] = s_ref[r]
            for h in range(DIL_HPG):
                on_scr[gi, h, rows, :] = o_ref[r, :, h * 128:(h + 1) * 128]
    s0, s1, s2 = sn_scr[0], sn_scr[1], sn_scr[2]
    for h in range(DIL_HPG):
        hs = slice(h * 128, (h + 1) * 128)
        m0, m1, m2 = s0[:, h:h + 1], s1[:, h:h + 1], s2[:, h:h + 1]
        l0, l1, l2 = (s0[:, DIL_HPG + h:DIL_HPG + h + 1], s1[:, DIL_HPG + h:DIL_HPG + h + 1],
                      s2[:, DIL_HPG + h:DIL_HPG + h + 1])
        mx = jnp.maximum(jnp.maximum(m0, m1), m2)
        w0 = jnp.exp2(m0 - mx) * l0
        w1 = jnp.exp2(m1 - mx) * l1
        w2 = jnp.exp2(m2 - mx) * l2
        num = w0 * on_scr[0, h] + w1 * on_scr[1, h] + w2 * on_scr[2, h]
        out_ref[:, hs] = (num / (w0 + w1 + w2)).astype(out_ref.dtype)


def _dil_merge(outs, stats, batch, seq, bm=512):
    w = DIL_HPG * 128
    nsb = seq // bm
    dils = tuple(o.shape[1] for o in outs)

    def spec(dil, width):
        return pl.BlockSpec((None, dil, bm // dil, width), lambda b, i: (b, 0, i, 0))

    return pl.pallas_call(
        functools.partial(_dil_merge_kernel, dils=dils),
        grid=(batch, nsb),
        in_specs=[spec(d, w) for d in dils] + [spec(d, 128) for d in dils],
        out_specs=pl.BlockSpec((bm, w), lambda b, i: (b * nsb + i, 0)),
        out_shape=jax.ShapeDtypeStruct((batch * seq, w), BF16),
        scratch_shapes=[pltpu.VMEM((3, DIL_HPG, bm, 128), F32), pltpu.VMEM((3, bm, 128), F32)],
        compiler_params=_cparams(("parallel", "parallel")),
        name="dilated_merge",
    )(*outs, *stats)


NA_TQ = NA_QROWS * GRID_W
NA_TK = NA_KROWS * GRID_W
NA_HPS = 16


def _na_bias_tables(rpb, rows):
    heads = rpb.shape[0]
    cols = np.arange(GRID_W)
    cs = np.clip(cols - NA_COLS // 2, 0, GRID_W - NA_COLS)
    kc = np.arange(GRID_W)
    col_ok = (kc[None, :] >= cs[:, None]) & (kc[None, :] < cs[:, None] + NA_COLS)
    col_idx = np.clip(kc[None, :] - cols[:, None] + (NA_COLS - 1), 0, 2 * NA_COLS - 2)
    toe = jnp.where(col_ok[None, None], (rpb * LOG2E)[:, :, col_idx], NEG_INF)
    offs, oks = [], []
    for i0, ks in ((0, 0), (NA_QROWS * 2, NA_QROWS), (rows - NA_QROWS, rows - NA_KROWS)):
        qi = i0 + np.arange(NA_QROWS)
        rs = np.clip(qi - NA_ROWS // 2, 0, rows - NA_ROWS)
        kr = ks + np.arange(NA_KROWS)
        oks.append((kr[None, :] >= rs[:, None]) & (kr[None, :] < rs[:, None] + NA_ROWS))
        offs.append(ks - qi + (NA_ROWS - 1))
    offs, oks = np.stack(offs), np.stack(oks)
    idx = offs[:, :, None] + np.arange(NA_KROWS)[None, None, :]
    assert idx[oks].min() >= 0 and idx[oks].max() <= 2 * NA_ROWS - 2
    wide = jnp.concatenate([toe, toe], axis=-1)

    def table_kernel(t_ref, o_ref):
        left = lax.broadcasted_iota(jnp.int32, (GRID_W, 2 * GRID_W), 1) < GRID_W
        masked = jnp.full((GRID_W, 2 * GRID_W), NEG_INF, F32)
        for kind in range(3):
            @pl.when(pl.program_id(1) == kind)
            def _(kind=kind):
                for q in range(NA_QROWS):
                    for pr in range(NA_KROWS // 2):
                        a, b = 2 * pr, 2 * pr + 1
                        ta = t_ref[int(idx[kind, q, a])] if oks[kind, q, a] else masked
                        tb = t_ref[int(idx[kind, q, b])] if oks[kind, q, b] else masked
                        o_ref[q * GRID_W:(q + 1) * GRID_W, pr * 2 * GRID_W:(pr + 1) * 2 * GRID_W] = (
                            jnp.where(left, ta, tb))

    return pl.pallas_call(
        table_kernel,
        grid=(heads, 3),
        in_specs=[pl.BlockSpec((None, 2 * NA_ROWS - 1, GRID_W, 2 * GRID_W), lambda h, kind: (h, 0, 0, 0))],
        out_specs=pl.BlockSpec((None, None, NA_TQ, NA_TK), lambda h, kind: (kind, h, 0, 0)),
        out_shape=jax.ShapeDtypeStruct((3, heads, NA_TQ, NA_TK), F32),
        compiler_params=_cparams(("parallel", "arbitrary")),
        name="na_bias_table",
    )(wide)


def _na_kernel(q_ref, k0_ref, k1_ref, k2_ref, v0_ref, v1_ref, v2_ref, b_ref, o_ref):
    for h in range(NA_HPS):
        hs = slice(h * 128, (h + 1) * 128)
        q = q_ref[0, :, hs]
        k = jnp.concatenate([k0_ref[0, :, hs], k1_ref[0, :, hs], k2_ref[0, :, hs]], axis=0)
        v = jnp.concatenate([v0_ref[0, :, hs], v1_ref[0, :, hs], v2_ref[0, :, hs]], axis=0)
        s = _dot_nt(q, k) + b_ref[0, h]
        mx = jnp.max(s, axis=-1, keepdims=True)
        p = jnp.exp2(s - mx)
        den = jnp.sum(p, axis=-1, keepdims=True)
        o = _dot(p.astype(v.dtype), v) / den
        o_ref[0, :, hs] = o.astype(o_ref.dtype)


def _na_attention(qkv, bias, batch, seq):
    rows = seq // GRID_W
    nblk = rows // NA_QROWS
    nkb = NA_KROWS // NA_QROWS
    ng = NA_HEADS // NA_HPS
    w = NA_HPS * 128
    qv = qkv.reshape(batch, seq, qkv.shape[1])

    def kv_spec(part, off):
        return pl.BlockSpec(
            (1, NA_TQ, w),
            lambda b, g, rb: (b, jnp.clip(rb - 1, 0, nblk - nkb) + off, part * ng + g))

    def bias_map(b, g, rb):
        cfg = jnp.where(rb == 0, 0, jnp.where(rb == nblk - 1, 2, 1))
        return (cfg, g, 0, 0)

    o = pl.pallas_call(
        _na_kernel,
        grid=(batch, ng, nblk),
        in_specs=[pl.BlockSpec((1, NA_TQ, w), lambda b, g, rb: (b, rb, g)),
                  kv_spec(1, 0), kv_spec(1, 1), kv_spec(1, 2),
                  kv_spec(2, 0), kv_spec(2, 1), kv_spec(2, 2),
                  pl.BlockSpec((1, NA_HPS, NA_TQ, NA_TK), bias_map)],
        out_specs=pl.BlockSpec((1, NA_TQ, w), lambda b, g, rb: (b, rb, g)),
        out_shape=jax.ShapeDtypeStruct((batch, seq, NA_HEADS * 128), BF16),
        compiler_params=_cparams(("parallel", "parallel", "arbitrary")),
        name="na_attention",
    )(qv, qv, qv, qv, qv, qv, qv, bias)
    return o.reshape(batch * seq, NA_HEADS * 128)


def _ffn_up_kernel(h_ref, w_ref, a_ref, ap_ref, an_ref, cw_ref, cb_ref, o_ref, *, bm, seq):
    i = pl.program_id(0)
    u = _dot(h_ref[...], w_ref[...])
    a = a_ref[...]
    first = (i * bm) % seq == 0
    last = ((i + 1) * bm) % seq == 0
    prev_row = jnp.where(first, 0.0, ap_ref[7:8, :])
    next_row = jnp.where(last, 0.0, an_ref[0:1, :])
    ridx = lax.broadcasted_iota(jnp.int32, (bm, 1), 0)
    a_prev = jnp.where(ridx == 0, prev_row, pltpu.roll(a, 1, 0))
    a_next = jnp.where(ridx == bm - 1, next_row, pltpu.roll(a, bm - 1, 0))
    cw = cw_ref[...]
    ac = a_prev * cw[0:1] + a * cw[1:2] + a_next * cw[2:3] + cb_ref[...]
    o_ref[...] = (ac * jax.nn.sigmoid(ac) * u).astype(o_ref.dtype)


def _ffn_up(h, w_up, a, conv_w, conv_b, seq, bm=1024, bn=512):
    m, k = h.shape
    n = w_up.shape[1]
    rb = bm // 8
    nrb = m // 8
    return pl.pallas_call(
        functools.partial(_ffn_up_kernel, bm=bm, seq=seq),
        grid=(m // bm, n // bn),
        in_specs=[pl.BlockSpec((bm, k), lambda i, j: (i, 0)),
                  pl.BlockSpec((k, bn), lambda i, j: (0, j)),
                  pl.BlockSpec((bm, bn), lambda i, j: (i, j)),
                  pl.BlockSpec((8, bn), lambda i, j: (jnp.maximum(i * rb - 1, 0), j)),
                  pl.BlockSpec((8, bn), lambda i, j: (jnp.minimum((i + 1) * rb, nrb - 1), j)),
                  pl.BlockSpec((3, bn), lambda i, j: (0, j)),
                  pl.BlockSpec((1, bn), lambda i, j: (0, j))],
        out_specs=pl.BlockSpec((bm, bn), lambda i, j: (i, j)),
        out_shape=jax.ShapeDtypeStruct((m, n), BF16),
        compiler_params=_cparams(("parallel", "parallel")),
        name="ffn_up",
    )(h, w_up, a, a, a, conv_w, conv_b)


def _conv_ffn(x2d, seq, norm_g, w_gate, w_up, conv_w, conv_b, w_down, layer):
    pad = D_FF_PAD - D_FF
    wg = _cast_pad(w_gate, layer, cols_out=D_FF_PAD)
    wu = _cast_pad(w_up, layer, cols_out=D_FF_PAD)
    wd = _cast_pad(w_down, layer, rows_out=D_FF_PAD)
    cw = jnp.pad(conv_w, ((0, 0), (0, pad)))
    cb = jnp.pad(conv_b, ((0, pad),)).reshape(1, D_FF_PAD)
    h = _rmsnorm(x2d, norm_g)
    a = _mm(h, wg, F32, name="ffn_gate")
    g = _ffn_up(h, wu, a, cw, cb, seq)
    return _mm_res_acc(g, wd, x2d, bk=D_FF_PAD // 2, bn=512, name="ffn_down")


def _even_layer(x2d, batch, seq, e_norm, w_in, q_lora_g, kv_lora_g, w_uq, w_ukv, mla_qn, mla_kn,
                dil_qn, dil_kn, w_out, layer):
    c3 = MLA_Q_RANK + MLA_KV_RANK + MLA_ROPE
    w_lat = jnp.pad(w_in[:, :c3], ((0, 0), (0, LAT_PAD - c3))).astype(BF16)
    w_b = w_in[:, c3:].astype(BF16)
    hpad = MLA_HEAD_PAD - MLA_QK
    w_uq_p = jnp.pad(w_uq.reshape(MLA_Q_RANK, MLA_HEADS, MLA_QK), ((0, 0), (0, 0), (0, hpad)))
    w_uq_p = w_uq_p.reshape(MLA_Q_RANK, MLA_HEADS * MLA_HEAD_PAD).astype(BF16)
    gq_pad = jnp.pad(mla_qn * (MLA_QK ** -0.5 * LOG2E), (0, hpad)).reshape(1, MLA_HEAD_PAD)
    gk_pad = jnp.pad(mla_kn, (0, hpad)).reshape(1, MLA_HEAD_PAD)
    mla_tabs = _rope_tables(seq, MLA_ROPE, 128)
    dil_tabs = _rope_tables(seq, DIL_ROT, 128)

    h = _rmsnorm(x2d, e_norm)
    cq, ckv, kpe = _lat_proj(h, w_lat, q_lora_g, kv_lora_g)
    q = _mla_q_proj(cq, w_uq_p, gq_pad, mla_tabs, seq)
    k, v = _mla_kv_proj(ckv, w_ukv.astype(BF16), kpe, gk_pad, mla_tabs, seq)
    o_a = _mla_attention(q, k, v, batch, seq)

    outs, stats = [], []
    for g, (window, dil) in enumerate(DIL_CONFIGS):
        assert window // (2 * dil) == DIL_RADIUS
        qkv_g = _dil_proj(h, w_b, jnp.stack([dil_qn * (HEAD_DIM ** -0.5 * LOG2E), dil_kn]), dil_tabs,
                          batch, seq, g, dil)
        o, st = _dilated_group(qkv_g, g)
        outs.append(o)
        stats.append(st)
    o_b = _dil_merge(outs, stats, batch, seq)
    return _mm_res2(o_a, o_b, _cast_pad(w_out, layer), x2d, name="even_out_proj")


def _odd_layer(x2d, batch, seq, o_norm, w_qkv, qn, kn, rpb, w_out, layer):
    h = _rmsnorm(x2d, o_norm)
    qkv = _heads_proj(h, _cast_pad(w_qkv, layer),
                      jnp.stack([qn * (HEAD_DIM ** -0.5 * LOG2E), kn]), NA_HEADS * 128 // 1024,
                      name="na_qkv_proj")
    bias = _na_bias_tables(rpb, seq // GRID_W)
    o = _na_attention(qkv, bias, batch, seq)
    return _mm_res(o, _cast_pad(w_out, layer), x2d, name="odd_out_proj")


def kernel(x, e_norm, e_w_in, e_q_lora_norm, e_kv_lora_norm, e_w_uq, e_w_ukv, e_mla_q_norm,
           e_mla_k_norm, e_dil_q_norm, e_dil_k_norm, e_w_out, o_norm, o_w_qkv, o_q_norm, o_k_norm,
           o_rpb, o_w_out, f_norm, f_w_gate, f_w_up, f_conv_w, f_conv_b, f_w_down):
    batch, seq, d = x.shape
    x2d = x.reshape(batch * seq, d)
    depth = f_norm.shape[0]
    for layer in range(depth):
        i = layer // 2
        if layer % 2 == 0:
            x2d = _even_layer(x2d, batch, seq, e_norm[i], e_w_in[i], e_q_lora_norm[i],
                              e_kv_lora_norm[i], e_w_uq[i], e_w_ukv[i], e_mla_q_norm[i],
                              e_mla_k_norm[i], e_dil_q_norm[i], e_dil_k_norm[i], e_w_out, i)
        else:
            x2d = _odd_layer(x2d, batch, seq, o_norm[i], o_w_qkv, o_q_norm[i], o_k_norm[i],
                             o_rpb[i], o_w_out, i)
        x2d = _conv_ffn(x2d, seq, f_norm[layer], f_w_gate, f_w_up, f_conv_w[layer],
                        f_conv_b[layer], f_w_down, layer)
    return x2d.reshape(batch, seq, d)
```

```python
import functools

import jax
import jax.numpy as jnp
import numpy as np
from jax import lax
from jax.experimental import pallas as pl
from jax.experimental.pallas import tpu as pltpu

F32 = jnp.float32
BF16 = jnp.bfloat16

HEAD_DIM = 128
ROPE_THETA = 500000.0
NORM_EPS = 1e-6
NEG_INF = -1e30

MLA_HEADS = 16
MLA_Q_RANK = 896
MLA_KV_RANK = 512
MLA_NOPE = 128
MLA_ROPE = 64
MLA_QK = MLA_NOPE + MLA_ROPE
MLA_V = 128
MLA_HEAD_PAD = 256
MLA_V_PAD = 256
LOG2E = float(np.log2(np.e))
LAT_PAD = 1536

DIL_CONFIGS = ((128, 1), (512, 4), (2048, 16))
DIL_HPG = 8
DIL_HEADS = 24
DIL_ROT = HEAD_DIM // 4
DIL_RADIUS = 64

GRID_W = 64
NA_HEADS = 32
NA_ROWS = 8
NA_COLS = 16
NA_QROWS = 4
NA_KROWS = 12

D_FF = 11008
D_FF_PAD = 11264

VMEM_LIMIT_BYTES = 56 * 1024 * 1024


def _cparams(sem):
    return pltpu.CompilerParams(dimension_semantics=sem, vmem_limit_bytes=VMEM_LIMIT_BYTES)


def _dot(a, b):
    return jnp.dot(a, b, preferred_element_type=F32)


def _dot_nt(a, b):
    return lax.dot_general(a, b, (((1,), (1,)), ((), ())), preferred_element_type=F32)


def _rmsnorm_kernel(x_ref, g_ref, o_ref):
    x = x_ref[...]
    y = x * lax.rsqrt(jnp.mean(x * x, axis=-1, keepdims=True) + NORM_EPS)
    o_ref[...] = (y * g_ref[...]).astype(o_ref.dtype)


def _rmsnorm(x2d, g, bm=512):
    m, d = x2d.shape
    return pl.pallas_call(
        _rmsnorm_kernel,
        grid=(m // bm,),
        in_specs=[pl.BlockSpec((bm, d), lambda i: (i, 0)),
                  pl.BlockSpec((1, d), lambda i: (0, 0))],
        out_specs=pl.BlockSpec((bm, d), lambda i: (i, 0)),
        out_shape=jax.ShapeDtypeStruct((m, d), BF16),
        compiler_params=_cparams(("parallel",)),
        name="rmsnorm",
    )(x2d, g.reshape(1, d))


def _cast_kernel(w_ref, o_ref, *, n_in_blocks, cols):
    i = pl.program_id(0)

    @pl.when(i < n_in_blocks)
    def _():
        o_ref[:, :cols] = w_ref[...].astype(o_ref.dtype)
        if o_ref.shape[1] > cols:
            o_ref[:, cols:] = jnp.zeros((o_ref.shape[0], o_ref.shape[1] - cols), o_ref.dtype)

    @pl.when(i >= n_in_blocks)
    def _():
        o_ref[...] = jnp.zeros(o_ref.shape, o_ref.dtype)


def _cast_pad(w_stack, layer, rows_out=None, cols_out=None, br=256):
    _, rows, cols = w_stack.shape
    rows_out = rows_out or rows
    cols_out = cols_out or cols
    nin = rows // br
    return pl.pallas_call(
        functools.partial(_cast_kernel, n_in_blocks=nin, cols=cols),
        grid=(rows_out // br,),
        in_specs=[pl.BlockSpec((None, br, cols), lambda i: (layer, jnp.minimum(i, nin - 1), 0))],
        out_specs=pl.BlockSpec((br, cols_out), lambda i: (i, 0)),
        out_shape=jax.ShapeDtypeStruct((rows_out, cols_out), BF16),
        compiler_params=_cparams(("parallel",)),
        name="cast_pad",
    )(w_stack)


def _mm_kernel(a_ref, w_ref, o_ref):
    o_ref[...] = _dot(a_ref[...], w_ref[...]).astype(o_ref.dtype)


def _mm(a, w, out_dtype, bm=1024, bn=1024, name="mm"):
    m, k = a.shape
    n = w.shape[1]
    return pl.pallas_call(
        _mm_kernel,
        grid=(m // bm, n // bn),
        in_specs=[pl.BlockSpec((bm, k), lambda i, j: (i, 0)),
                  pl.BlockSpec((k, bn), lambda i, j: (0, j))],
        out_specs=pl.BlockSpec((bm, bn), lambda i, j: (i, j)),
        out_shape=jax.ShapeDtypeStruct((m, n), out_dtype),
        compiler_params=_cparams(("parallel", "parallel")),
        name=name,
    )(a, w)


def _mm_res_kernel(a_ref, w_ref, r_ref, o_ref):
    o_ref[...] = r_ref[...] + _dot(a_ref[...], w_ref[...])


def _mm_res(a, w, res, bm=1024, bn=512, name="mm_res"):
    m, k = a.shape
    n = w.shape[1]
    return pl.pallas_call(
        _mm_res_kernel,
        grid=(m // bm, n // bn),
        in_specs=[pl.BlockSpec((bm, k), lambda i, j: (i, 0)),
                  pl.BlockSpec((k, bn), lambda i, j: (0, j)),
                  pl.BlockSpec((bm, bn), lambda i, j: (i, j))],
        out_specs=pl.BlockSpec((bm, bn), lambda i, j: (i, j)),
        out_shape=jax.ShapeDtypeStruct((m, n), F32),
        compiler_params=_cparams(("parallel", "parallel")),
        name=name,
    )(a, w, res)


def _mm_res2_kernel(a1_ref, a2_ref, w1_ref, w2_ref, r_ref, o_ref):
    o_ref[...] = r_ref[...] + _dot(a1_ref[...], w1_ref[...]) + _dot(a2_ref[...], w2_ref[...])


def _mm_res2(a1, a2, w, res, bm=1024, bn=512, name="mm_res2"):
    m, k1 = a1.shape
    k2 = a2.shape[1]
    n = w.shape[1]
    assert k1 % k2 == 0 and w.shape[0] == k1 + k2
    return pl.pallas_call(
        _mm_res2_kernel,
        grid=(m // bm, n // bn),
        in_specs=[pl.BlockSpec((bm, k1), lambda i, j: (i, 0)),
                  pl.BlockSpec((bm, k2), lambda i, j: (i, 0)),
                  pl.BlockSpec((k1, bn), lambda i, j: (0, j)),
                  pl.BlockSpec((k2, bn), lambda i, j: (k1 // k2, j)),
                  pl.BlockSpec((bm, bn), lambda i, j: (i, j))],
        out_specs=pl.BlockSpec((bm, bn), lambda i, j: (i, j)),
        out_shape=jax.ShapeDtypeStruct((m, n), F32),
        compiler_params=_cparams(("parallel", "parallel")),
        name=name,
    )(a1, a2, w, w, res)


def _mm_res_acc_kernel(a_ref, w_ref, r_ref, o_ref, acc_ref):
    k = pl.program_id(2)

    @pl.when(k == 0)
    def _():
        acc_ref[...] = jnp.zeros(acc_ref.shape, F32)

    acc_ref[...] += _dot(a_ref[...], w_ref[...])

    @pl.when(k == pl.num_programs(2) - 1)
    def _():
        o_ref[...] = r_ref[...] + acc_ref[...]


def _mm_res_acc(a, w, res, bk, bm=1024, bn=1024, name="mm_res_acc"):
    m, k = a.shape
    n = w.shape[1]
    return pl.pallas_call(
        _mm_res_acc_kernel,
        grid=(m // bm, n // bn, k // bk),
        in_specs=[pl.BlockSpec((bm, bk), lambda i, j, kk: (i, kk)),
                  pl.BlockSpec((bk, bn), lambda i, j, kk: (kk, j)),
                  pl.BlockSpec((bm, bn), lambda i, j, kk: (i, j))],
        out_specs=pl.BlockSpec((bm, bn), lambda i, j, kk: (i, j)),
        out_shape=jax.ShapeDtypeStruct((m, n), F32),
        scratch_shapes=[pltpu.VMEM((bm, bn), F32)],
        compiler_params=_cparams(("parallel", "parallel", "arbitrary")),
        name=name,
    )(a, w, res)


def _rope_tables(seq, rot, width):
    half = rot // 2
    inv_freq = 1.0 / (ROPE_THETA ** (jnp.arange(half, dtype=F32) * (2.0 / rot)))
    ang = jnp.arange(seq, dtype=jnp.int32).astype(F32)[:, None] * inv_freq[None, :]
    cos, sin = jnp.cos(ang), jnp.sin(ang)
    zh = jnp.zeros((seq, half), F32)
    rest = width - rot
    c = jnp.concatenate([cos, cos, jnp.ones((seq, rest), F32)], axis=1)
    sa = jnp.concatenate([zh, sin, jnp.zeros((seq, rest), F32)], axis=1)
    sb = jnp.concatenate([-sin, zh, jnp.zeros((seq, rest), F32)], axis=1)
    return c, sa, sb


def _apply_rope(y, c, sa, sb, half):
    width = y.shape[-1]
    return y * c + pltpu.roll(y, half, 1) * sa + pltpu.roll(y, width - half, 1) * sb


def _lat_kernel(x_ref, gn_ref, w_ref, gq_ref, gkv_ref, h_ref, cq_ref, ckv_ref, kpe_ref):
    x = x_ref[...]
    y = x * lax.rsqrt(jnp.mean(x * x, axis=-1, keepdims=True) + NORM_EPS)
    h = (y * gn_ref[...]).astype(h_ref.dtype)
    h_ref[...] = h
    z = _dot(h, w_ref[...])
    c1 = MLA_Q_RANK
    c2 = c1 + MLA_KV_RANK
    cq = z[:, :c1]
    ckv = z[:, c1:c2]
    cq = cq * lax.rsqrt(jnp.mean(cq * cq, axis=-1, keepdims=True) + NORM_EPS)
    ckv = ckv * lax.rsqrt(jnp.mean(ckv * ckv, axis=-1, keepdims=True) + NORM_EPS)
    cq_ref[...] = (cq * gq_ref[...]).astype(cq_ref.dtype)
    ckv_ref[...] = (ckv * gkv_ref[...]).astype(ckv_ref.dtype)
    kpe_ref[...] = z[:, c2:]


def _lat_proj(x2d, norm_g, w_lat, gq, gkv, bm=512):
    m, k = x2d.shape
    return pl.pallas_call(
        _lat_kernel,
        grid=(m // bm,),
        in_specs=[pl.BlockSpec((bm, k), lambda i: (i, 0)),
                  pl.BlockSpec((1, k), lambda i: (0, 0)),
                  pl.BlockSpec((k, LAT_PAD), lambda i: (0, 0), pipeline_mode=pl.Buffered(1)),
                  pl.BlockSpec((1, MLA_Q_RANK), lambda i: (0, 0)),
                  pl.BlockSpec((1, MLA_KV_RANK), lambda i: (0, 0))],
        out_specs=[pl.BlockSpec((bm, k), lambda i: (i, 0)),
                   pl.BlockSpec((bm, MLA_Q_RANK), lambda i: (i, 0)),
                   pl.BlockSpec((bm, MLA_KV_RANK), lambda i: (i, 0)),
                   pl.BlockSpec((bm, 128), lambda i: (i, 0))],
        out_shape=[jax.ShapeDtypeStruct((m, k), BF16),
                   jax.ShapeDtypeStruct((m, MLA_Q_RANK), BF16),
                   jax.ShapeDtypeStruct((m, MLA_KV_RANK), BF16),
                   jax.ShapeDtypeStruct((m, 128), F32)],
        compiler_params=_cparams(("parallel",)),
        name="lat_proj",
    )(x2d, norm_g.reshape(1, k), w_lat, gq.reshape(1, -1), gkv.reshape(1, -1))


def _mla_q_kernel(a_ref, w_ref, g_ref, c_ref, sa_ref, sb_ref, o_ref, *, heads):
    z = _dot(a_ref[...], w_ref[...])
    g = g_ref[...]
    c, sa, sb = c_ref[...], sa_ref[...], sb_ref[...]
    for h in range(heads):
        zh = z[:, h * MLA_HEAD_PAD:(h + 1) * MLA_HEAD_PAD]
        ms = jnp.sum(zh * zh, axis=-1, keepdims=True) * (1.0 / MLA_QK)
        y = zh * lax.rsqrt(ms + NORM_EPS) * g
        o_ref[:, h * MLA_HEAD_PAD:h * MLA_HEAD_PAD + 128] = y[:, :128].astype(o_ref.dtype)
        yr = _apply_rope(y[:, 128:], c, sa, sb, MLA_ROPE // 2)
        o_ref[:, h * MLA_HEAD_PAD + 128:(h + 1) * MLA_HEAD_PAD] = yr.astype(o_ref.dtype)


def _mla_q_proj(cq, w_uq, g_pad, tabs, seq, bm=1024, heads=4):
    m, k = cq.shape
    n = w_uq.shape[1]
    bn = heads * MLA_HEAD_PAD
    nsb = seq // bm
    tab_spec = pl.BlockSpec((bm, 128), lambda i, j: (i % nsb, 0))
    return pl.pallas_call(
        functools.partial(_mla_q_kernel, heads=heads),
        grid=(m // bm, n // bn),
        in_specs=[pl.BlockSpec((bm, k), lambda i, j: (i, 0)),
                  pl.BlockSpec((k, bn), lambda i, j: (0, j)),
                  pl.BlockSpec((1, MLA_HEAD_PAD), lambda i, j: (0, 0)),
                  tab_spec, tab_spec, tab_spec],
        out_specs=pl.BlockSpec((bm, bn), lambda i, j: (i, j)),
        out_shape=jax.ShapeDtypeStruct((m, n), BF16),
        compiler_params=_cparams(("parallel", "parallel")),
        name="mla_q_proj",
    )(cq, w_uq, g_pad, *tabs)


def _mla_kv_kernel(a_ref, w_ref, kpe_ref, g_ref, c_ref, sa_ref, sb_ref, k_ref, v_ref, *, heads):
    z = _dot(a_ref[...], w_ref[...])
    kpe = kpe_ref[...]
    g = g_ref[...]
    g_nope, g_rope = g[:, :128], g[:, 128:]
    c, sa, sb = c_ref[...], sa_ref[...], sb_ref[...]
    ss_pe = jnp.sum(kpe * kpe, axis=-1, keepdims=True)
    one_col = (lax.broadcasted_iota(jnp.int32, (1, 128), 1) == 0).astype(v_ref.dtype)
    kpe_rot = _apply_rope(kpe * g_rope, c, sa, sb, MLA_ROPE // 2)
    for h in range(heads):
        nope = z[:, h * 256:h * 256 + 128]
        ms = (jnp.sum(nope * nope, axis=-1, keepdims=True) + ss_pe) * (1.0 / MLA_QK)
        r = lax.rsqrt(ms + NORM_EPS)
        k_ref[:, h * MLA_HEAD_PAD:h * MLA_HEAD_PAD + 128] = (nope * r * g_nope).astype(k_ref.dtype)
        k_ref[:, h * MLA_HEAD_PAD + 128:(h + 1) * MLA_HEAD_PAD] = (kpe_rot * r).astype(k_ref.dtype)
        v_ref[:, h * MLA_V_PAD:h * MLA_V_PAD + MLA_V] = z[:, h * 256 + 128:(h + 1) * 256].astype(v_ref.dtype)
        v_ref[:, h * MLA_V_PAD + MLA_V:(h + 1) * MLA_V_PAD] = jnp.broadcast_to(one_col, (z.shape[0], 128))


def _mla_kv_proj(ckv, w_ukv, kpe, g_pad, tabs, seq, bm=1024, heads=4):
    m, k = ckv.shape
    nsb = seq // bm
    tab_spec = pl.BlockSpec((bm, 128), lambda i, j: (i % nsb, 0))
    return pl.pallas_call(
        functools.partial(_mla_kv_kernel, heads=heads),
        grid=(m // bm, MLA_HEADS // heads),
        in_specs=[pl.BlockSpec((bm, k), lambda i, j: (i, 0)),
                  pl.BlockSpec((k, heads * 256), lambda i, j: (0, j)),
                  pl.BlockSpec((bm, 128), lambda i, j: (i, 0)),
                  pl.BlockSpec((1, MLA_HEAD_PAD), lambda i, j: (0, 0)),
                  tab_spec, tab_spec, tab_spec],
        out_specs=[pl.BlockSpec((bm, heads * MLA_HEAD_PAD), lambda i, j: (i, j)),
                   pl.BlockSpec((bm, heads * MLA_V_PAD), lambda i, j: (i, j))],
        out_shape=[jax.ShapeDtypeStruct((m, MLA_HEADS * MLA_HEAD_PAD), BF16),
                   jax.ShapeDtypeStruct((m, MLA_HEADS * MLA_V_PAD), BF16)],
        compiler_params=_cparams(("parallel", "parallel")),
        name="mla_kv_proj",
    )(ckv, w_ukv, kpe, g_pad, *tabs)


def _heads_kernel(a_ref, w_ref, g_ref, o_ref, *, heads, n_norm):
    z = _dot(a_ref[...], w_ref[...])
    j = pl.program_id(1)

    @pl.when(j < n_norm)
    def _():
        g = g_ref[0]
        for h in range(heads):
            zh = z[:, h * 128:(h + 1) * 128]
            y = zh * lax.rsqrt(jnp.mean(zh * zh, axis=-1, keepdims=True) + NORM_EPS) * g
            o_ref[:, h * 128:(h + 1) * 128] = y.astype(o_ref.dtype)

    @pl.when(j >= n_norm)
    def _():
        o_ref[...] = z.astype(o_ref.dtype)


def _heads_proj(a, w, gains, n_q_blocks, bm=1024, bn=1024, name="heads_proj"):
    m, k = a.shape
    n = w.shape[1]
    g3 = jnp.concatenate([gains, jnp.ones((1, 128), F32)], axis=0).reshape(3, 1, 128)
    return pl.pallas_call(
        functools.partial(_heads_kernel, heads=bn // 128, n_norm=2 * n_q_blocks),
        grid=(m // bm, n // bn),
        in_specs=[pl.BlockSpec((bm, k), lambda i, j: (i, 0)),
                  pl.BlockSpec((k, bn), lambda i, j: (0, j)),
                  pl.BlockSpec((1, 1, 128), lambda i, j: (j // n_q_blocks, 0, 0))],
        out_specs=pl.BlockSpec((bm, bn), lambda i, j: (i, j)),
        out_shape=jax.ShapeDtypeStruct((m, n), BF16),
        compiler_params=_cparams(("parallel", "parallel")),
        name=name,
    )(a, w, g3)


def _flash_stages(q_ref, k_ref, v_ref, o_ref, s_w, s_r, p_w, p_r, al_w, al_r, mr_w, mr_r, m_ref,
                  acc_ref, *, total, nk):
    t = pl.program_id(0)
    s = _dot_nt(q_ref[...], k_ref[...])
    s_w[...] = s
    mr_w[...] = jnp.max(s, axis=-1, keepdims=True)

    valid_b = jnp.logical_and(t >= 1, t <= total)
    ki_b = jnp.clip(t - 1, 0, total - 1) % nk
    m_old = m_ref[...]
    m_prev = jnp.where(ki_b == 0, -jnp.inf, m_old)
    m_new = jnp.maximum(m_prev, mr_r[...])
    al_w[...] = jnp.exp2(m_prev - m_new)
    p_w[...] = jnp.exp2(s_r[...] - m_new).astype(p_w.dtype)
    m_ref[...] = jnp.where(valid_b, m_new, m_old)

    acc = al_r[...] * acc_ref[...] + _dot(p_r[...], v_ref[...])
    acc_ref[...] = acc
    ki_c = jnp.clip(t - 2, 0, total - 1) % nk

    @pl.when(jnp.logical_and(t >= 2, ki_c == nk - 1))
    def _():
        o_ref[...] = (acc[:, :MLA_V] / acc[:, MLA_V:MLA_V + 1]).astype(o_ref.dtype)


def _flash_kernel(q_ref, k_ref, v_ref, o_ref, s0, s1, p0, p1, al0, al1, mr0, mr1, m_ref, acc_ref, **kw):
    t = pl.program_id(0)
    s_scr, p_scr, al_scr, mr_scr = (s0, s1), (p0, p1), (al0, al1), (mr0, mr1)

    @pl.when(t == 0)
    def _():
        for ref in (s0, s1, p0, p1, al0, al1, mr0, mr1, m_ref, acc_ref):
            ref[...] = jnp.zeros(ref.shape, ref.dtype)

    for par in (0, 1):
        @pl.when(t % 2 == par)
        def _(par=par):
            _flash_stages(q_ref, k_ref, v_ref, o_ref, s_scr[par], s_scr[1 - par],
                          p_scr[1 - par], p_scr[par], al_scr[1 - par], al_scr[par],
                          mr_scr[par], mr_scr[1 - par], m_ref, acc_ref, **kw)


def _mla_attention(q, k, v, batch, seq, tq=1024, tk=2048):
    m = q.shape[0]
    tk = min(tk, seq // 2)
    nq, nk = seq // tq, seq // tk
    total = batch * MLA_HEADS * nq * nk

    def unravel(tt):
        ki = tt % nk
        r = tt // nk
        qi = r % nq
        r = r // nq
        return r // MLA_HEADS, r % MLA_HEADS, qi, ki

    def q_map(t):
        b, h, qi, _ = unravel(jnp.minimum(t, total - 1))
        return (b * nq + qi, h)

    def k_map(t):
        b, h, _, ki = unravel(jnp.minimum(t, total - 1))
        return (b * nk + ki, h)

    def v_map(t):
        b, h, _, ki = unravel(jnp.clip(t - 2, 0, total - 1))
        return (b * nk + ki, h)

    def o_map(t):
        b, h, qi, _ = unravel(jnp.clip(t - 2, 0, total - 1))
        return (b * nq + qi, h)

    def col():
        return pltpu.VMEM((tq, 1), F32)

    return pl.pallas_call(
        functools.partial(_flash_kernel, total=total, nk=nk),
        grid=(total + 2,),
        in_specs=[pl.BlockSpec((tq, MLA_HEAD_PAD), q_map),
                  pl.BlockSpec((tk, MLA_HEAD_PAD), k_map),
                  pl.BlockSpec((tk, MLA_V_PAD), v_map)],
        out_specs=pl.BlockSpec((tq, MLA_V), o_map),
        out_shape=jax.ShapeDtypeStruct((m, MLA_HEADS * MLA_V), BF16),
        scratch_shapes=[pltpu.VMEM((tq, tk), F32), pltpu.VMEM((tq, tk), F32),
                        pltpu.VMEM((tq, tk), BF16), pltpu.VMEM((tq, tk), BF16),
                        col(), col(), col(), col(), col(), pltpu.VMEM((tq, MLA_V_PAD), F32)],
        compiler_params=_cparams(("arbitrary",)),
        name="mla_attention",
    )(q, k, v)


DIL_SQ = 256
DIL_SK = DIL_SQ + 2 * DIL_RADIUS


def _dil_kernel(q_ref, kp_ref, k_ref, kn_ref, vp_ref, v_ref, vn_ref, o_ref, st_ref,
                kwin_ref, vwin_ref, *, tq, length):
    r = DIL_RADIUS
    q0 = pl.program_id(2) * tq
    kwin_ref[0:r] = kp_ref[...]
    kwin_ref[r:r + tq] = k_ref[...]
    kwin_ref[r + tq:] = kn_ref[...]
    vwin_ref[0:r] = vp_ref[...]
    vwin_ref[r:r + tq] = v_ref[...]
    vwin_ref[r + tq:] = vn_ref[...]
    qi = lax.broadcasted_iota(jnp.int32, (DIL_SQ, DIL_SK), 0)
    kj = lax.broadcasted_iota(jnp.int32, (DIL_SQ, DIL_SK), 1)
    band = (kj >= qi) & (kj <= qi + 2 * r)
    lane = lax.broadcasted_iota(jnp.int32, (1, 128), 1)
    for c in range(tq // DIL_SQ):
        kpos = q0 + c * DIL_SQ - r + kj
        valid = band & (kpos >= 0) & (kpos < length)
        st = jnp.zeros((DIL_SQ, 128), F32)
        for h in range(DIL_HPG):
            hs = slice(h * 128, (h + 1) * 128)
            qh = q_ref[c * DIL_SQ:(c + 1) * DIL_SQ, hs]
            kh = kwin_ref[c * DIL_SQ:c * DIL_SQ + DIL_SK, hs]
            vh = vwin_ref[c * DIL_SQ:c * DIL_SQ + DIL_SK, hs]
            s = jnp.where(valid, _dot_nt(qh, kh), NEG_INF)
            mx = jnp.max(s, axis=-1, keepdims=True)
            p = jnp.exp2(s - mx)
            den = jnp.sum(p, axis=-1, keepdims=True)
            o = _dot(p.astype(vh.dtype), vh) / den
            o_ref[c * DIL_SQ:(c + 1) * DIL_SQ, hs] = o
            st = jnp.where(lane == h, mx, st)
            st = jnp.where(lane == DIL_HPG + h, den, st)
        st_ref[c * DIL_SQ:(c + 1) * DIL_SQ, :] = st


def _dil_proj_kernel(a_ref, w_ref, g_ref, c_ref, sa_ref, sb_ref, o_ref, y_scr, *, dil, heads):
    z = _dot(a_ref[...], w_ref[...])
    part = pl.program_id(1)
    rows = z.shape[0] // dil

    @pl.when(part < 2)
    def _():
        g = g_ref[0]
        for h in range(heads):
            zh = z[:, h * 128:(h + 1) * 128]
            y = zh * lax.rsqrt(jnp.mean(zh * zh, axis=-1, keepdims=True) + NORM_EPS) * g
            y_scr[h] = _apply_rope(y, c_ref[...], sa_ref[...], sb_ref[...], DIL_ROT // 2)

    @pl.when(part == 2)
    def _():
        for h in range(heads):
            y_scr[h] = z[:, h * 128:(h + 1) * 128]

    for h in range(heads):
        hs = slice(h * 128, (h + 1) * 128)
        if dil == 1:
            o_ref[0, :, hs] = y_scr[h].astype(o_ref.dtype)
        else:
            for r in range(dil):
                o_ref[r, :, hs] = y_scr[h, pl.ds(r, rows, stride=dil), :].astype(o_ref.dtype)


def _dil_proj(h, w_b, gains, tabs, batch, seq, group, dil, bm=1024):
    m, k = h.shape
    w = DIL_HPG * 128
    nsb = seq // bm
    g3 = jnp.concatenate([gains, jnp.ones((1, 128), F32)], axis=0).reshape(3, 1, 128)
    tab_spec = pl.BlockSpec((bm, 128), lambda i, p: (i % nsb, 0))
    return pl.pallas_call(
        functools.partial(_dil_proj_kernel, dil=dil, heads=DIL_HPG),
        grid=(m // bm, 3),
        in_specs=[pl.BlockSpec((bm, k), lambda i, p: (i, 0)),
                  pl.BlockSpec((k, w), lambda i, p: (0, p * 3 + group)),
                  pl.BlockSpec((1, 1, 128), lambda i, p: (p, 0, 0)),
                  tab_spec, tab_spec, tab_spec],
        out_specs=pl.BlockSpec((None, dil, bm // dil, w), lambda i, p: (i // nsb, 0, i % nsb, p)),
        out_shape=jax.ShapeDtypeStruct((batch, dil, seq // dil, 3 * w), BF16),
        scratch_shapes=[pltpu.VMEM((DIL_HPG, bm, 128), F32)],
        compiler_params=_cparams(("parallel", "arbitrary")),
        name=f"dil_proj_g{group}",
    )(h, w_b, g3, *tabs)


def _dilated_group(qkv, group):
    batch, dil, length, _ = qkv.shape
    tq = min(512, length)
    r = DIL_RADIUS
    nq = length // tq
    hb = tq // r
    nhb = length // r
    w = DIL_HPG * 128

    def main(part):
        return pl.BlockSpec((None, None, tq, w), lambda b, rr, qi: (b, rr, qi, part))

    def prev(part):
        return pl.BlockSpec((None, None, r, w),
                            lambda b, rr, qi: (b, rr, jnp.maximum(qi * hb - 1, 0), part))

    def nxt(part):
        return pl.BlockSpec((None, None, r, w),
                            lambda b, rr, qi: (b, rr, jnp.minimum((qi + 1) * hb, nhb - 1), part))

    return pl.pallas_call(
        functools.partial(_dil_kernel, tq=tq, length=length),
        grid=(batch, dil, nq),
        in_specs=[main(0), prev(1), main(1), nxt(1), prev(2), main(2), nxt(2)],
        out_specs=[pl.BlockSpec((None, None, tq, w), lambda b, rr, qi: (b, rr, qi, 0)),
                   pl.BlockSpec((None, None, tq, 128), lambda b, rr, qi: (b, rr, qi, 0))],
        out_shape=[jax.ShapeDtypeStruct((batch, dil, length, w), F32),
                   jax.ShapeDtypeStruct((batch, dil, length, 128), F32)],
        scratch_shapes=[pltpu.VMEM((tq + 2 * r, w), BF16), pltpu.VMEM((tq + 2 * r, w), BF16)],
        compiler_params=_cparams(("parallel", "parallel", "parallel")),
        name=f"dilated_g{group}",
    )(qkv, qkv, qkv, qkv, qkv, qkv, qkv)


def _dil_merge_kernel(o0_ref, o1_ref, o2_ref, s0_ref, s1_ref, s2_ref, out_ref, on_scr, sn_scr,
                      *, dils):
    bm = out_ref.shape[0]
    for gi, (o_ref, s_ref) in enumerate(((o0_ref, s0_ref), (o1_ref, s1_ref), (o2_ref, s2_ref))):
        dil = dils[gi]
        for r in range(dil):
            rows = pl.ds(r, bm // dil, stride=dil) if dil > 1 else slice(None)
            sn_scr[gi, rows, :] = s_ref[r]
            for h in range(DIL_HPG):
                on_scr[gi, h, rows, :] = o_ref[r, :, h * 128:(h + 1) * 128]
    s0, s1, s2 = sn_scr[0], sn_scr[1], sn_scr[2]
    for h in range(DIL_HPG):
        hs = slice(h * 128, (h + 1) * 128)
        m0, m1, m2 = s0[:, h:h + 1], s1[:, h:h + 1], s2[:, h:h + 1]
        l0, l1, l2 = (s0[:, DIL_HPG + h:DIL_HPG + h + 1], s1[:, DIL_HPG + h:DIL_HPG + h + 1],
                      s2[:, DIL_HPG + h:DIL_HPG + h + 1])
        mx = jnp.maximum(jnp.maximum(m0, m1), m2)
        w0 = jnp.exp2(m0 - mx) * l0
        w1 = jnp.exp2(m1 - mx) * l1
        w2 = jnp.exp2(m2 - mx) * l2
        num = w0 * on_scr[0, h] + w1 * on_scr[1, h] + w2 * on_scr[2, h]
        out_ref[:, hs] = (num / (w0 + w1 + w2)).astype(out_ref.dtype)


def _dil_merge(outs, stats, batch, seq, bm=512):
    w = DIL_HPG * 128
    nsb = seq // bm
    dils = tuple(o.shape[1] for o in outs)

    def spec(dil, width):
        return pl.BlockSpec((None, dil, bm // dil, width), lambda b, i: (b, 0, i, 0))

    return pl.pallas_call(
        functools.partial(_dil_merge_kernel, dils=dils),
        grid=(batch, nsb),
        in_specs=[spec(d, w) for d in dils] + [spec(d, 128) for d in dils],
        out_specs=pl.BlockSpec((bm, w), lambda b, i: (b * nsb + i, 0)),
        out_shape=jax.ShapeDtypeStruct((batch * seq, w), BF16),
        scratch_shapes=[pltpu.VMEM((3, DIL_HPG, bm, 128), F32), pltpu.VMEM((3, bm, 128), F32)],
        compiler_params=_cparams(("parallel", "parallel")),
        name="dilated_merge",
    )(*outs, *stats)


NA_TQ = NA_QROWS * GRID_W
NA_TK = NA_KROWS * GRID_W
NA_HPS = 16


def _na_bias_tables(rpb, rows):
    heads = rpb.shape[0]
    cols = np.arange(GRID_W)
    cs = np.clip(cols - NA_COLS // 2, 0, GRID_W - NA_COLS)
    kc = np.arange(GRID_W)
    col_ok = (kc[None, :] >= cs[:, None]) & (kc[None, :] < cs[:, None] + NA_COLS)
    col_idx = np.clip(kc[None, :] - cols[:, None] + (NA_COLS - 1), 0, 2 * NA_COLS - 2)
    toe = jnp.where(col_ok[None, None], (rpb * LOG2E)[:, :, col_idx], NEG_INF)
    offs, oks = [], []
    for i0, ks in ((0, 0), (NA_QROWS * 2, NA_QROWS), (rows - NA_QROWS, rows - NA_KROWS)):
        qi = i0 + np.arange(NA_QROWS)
        rs = np.clip(qi - NA_ROWS // 2, 0, rows - NA_ROWS)
        kr = ks + np.arange(NA_KROWS)
        oks.append((kr[None, :] >= rs[:, None]) & (kr[None, :] < rs[:, None] + NA_ROWS))
        offs.append(ks - qi + (NA_ROWS - 1))
    offs, oks = np.stack(offs), np.stack(oks)
    idx = offs[:, :, None] + np.arange(NA_KROWS)[None, None, :]
    assert idx[oks].min() >= 0 and idx[oks].max() <= 2 * NA_ROWS - 2
    wide = jnp.concatenate([toe, toe], axis=-1)

    def table_kernel(t_ref, o_ref):
        left = lax.broadcasted_iota(jnp.int32, (GRID_W, 2 * GRID_W), 1) < GRID_W
        masked = jnp.full((GRID_W, 2 * GRID_W), NEG_INF, F32)
        for kind in range(3):
            @pl.when(pl.program_id(1) == kind)
            def _(kind=kind):
                for q in range(NA_QROWS):
                    for pr in range(NA_KROWS // 2):
                        a, b = 2 * pr, 2 * pr + 1
                        ta = t_ref[int(idx[kind, q, a])] if oks[kind, q, a] else masked
                        tb = t_ref[int(idx[kind, q, b])] if oks[kind, q, b] else masked
                        o_ref[q * GRID_W:(q + 1) * GRID_W, pr * 2 * GRID_W:(pr + 1) * 2 * GRID_W] = (
                            jnp.where(left, ta, tb))

    return pl.pallas_call(
        table_kernel,
        grid=(heads, 3),
        in_specs=[pl.BlockSpec((None, 2 * NA_ROWS - 1, GRID_W, 2 * GRID_W), lambda h, kind: (h, 0, 0, 0))],
        out_specs=pl.BlockSpec((None, None, NA_TQ, NA_TK), lambda h, kind: (kind, h, 0, 0)),
        out_shape=jax.ShapeDtypeStruct((3, heads, NA_TQ, NA_TK), F32),
        compiler_params=_cparams(("parallel", "arbitrary")),
        name="na_bias_table",
    )(wide)


def _na_kernel(q_ref, k0_ref, k1_ref, k2_ref, v0_ref, v1_ref, v2_ref, b_ref, o_ref):
    for h in range(NA_HPS):
        hs = slice(h * 128, (h + 1) * 128)
        q = q_ref[0, :, hs]
        k = jnp.concatenate([k0_ref[0, :, hs], k1_ref[0, :, hs], k2_ref[0, :, hs]], axis=0)
        v = jnp.concatenate([v0_ref[0, :, hs], v1_ref[0, :, hs], v2_ref[0, :, hs]], axis=0)
        s = _dot_nt(q, k) + b_ref[0, h]
        mx = jnp.max(s, axis=-1, keepdims=True)
        p = jnp.exp2(s - mx)
        den = jnp.sum(p, axis=-1, keepdims=True)
        o = _dot(p.astype(v.dtype), v) / den
        o_ref[0, :, hs] = o.astype(o_ref.dtype)


def _na_attention(qkv, bias, batch, seq):
    rows = seq // GRID_W
    nblk = rows // NA_QROWS
    nkb = NA_KROWS // NA_QROWS
    ng = NA_HEADS // NA_HPS
    w = NA_HPS * 128
    qv = qkv.reshape(batch, seq, qkv.shape[1])

    def kv_spec(part, off):
        return pl.BlockSpec(
            (1, NA_TQ, w),
            lambda b, g, rb: (b, jnp.clip(rb - 1, 0, nblk - nkb) + off, part * ng + g))

    def bias_map(b, g, rb):
        cfg = jnp.where(rb == 0, 0, jnp.where(rb == nblk - 1, 2, 1))
        return (cfg, g, 0, 0)

    o = pl.pallas_call(
        _na_kernel,
        grid=(batch, ng, nblk),
        in_specs=[pl.BlockSpec((1, NA_TQ, w), lambda b, g, rb: (b, rb, g)),
                  kv_spec(1, 0), kv_spec(1, 1), kv_spec(1, 2),
                  kv_spec(2, 0), kv_spec(2, 1), kv_spec(2, 2),
                  pl.BlockSpec((1, NA_HPS, NA_TQ, NA_TK), bias_map)],
        out_specs=pl.BlockSpec((1, NA_TQ, w), lambda b, g, rb: (b, rb, g)),
        out_shape=jax.ShapeDtypeStruct((batch, seq, NA_HEADS * 128), BF16),
        compiler_params=_cparams(("parallel", "parallel", "arbitrary")),
        name="na_attention",
    )(qv, qv, qv, qv, qv, qv, qv, bias)
    return o.reshape(batch * seq, NA_HEADS * 128)


def _ffn_up_kernel(h_ref, w_ref, a_ref, ap_ref, an_ref, cw_ref, cb_ref, o_ref, *, bm, seq):
    i = pl.program_id(0)
    u = _dot(h_ref[...], w_ref[...])
    a = a_ref[...]
    first = (i * bm) % seq == 0
    last = ((i + 1) * bm) % seq == 0
    prev_row = jnp.where(first, 0.0, ap_ref[7:8, :])
    next_row = jnp.where(last, 0.0, an_ref[0:1, :])
    ridx = lax.broadcasted_iota(jnp.int32, (bm, 1), 0)
    a_prev = jnp.where(ridx == 0, prev_row, pltpu.roll(a, 1, 0))
    a_next = jnp.where(ridx == bm - 1, next_row, pltpu.roll(a, bm - 1, 0))
    cw = cw_ref[...]
    ac = a_prev * cw[0:1] + a * cw[1:2] + a_next * cw[2:3] + cb_ref[...]
    o_ref[...] = (ac * jax.nn.sigmoid(ac) * u).astype(o_ref.dtype)


def _ffn_up(h, w_up, a, conv_w, conv_b, seq, bm=1024, bn=512):
    m, k = h.shape
    n = w_up.shape[1]
    rb = bm // 8
    nrb = m // 8
    return pl.pallas_call(
        functools.partial(_ffn_up_kernel, bm=bm, seq=seq),
        grid=(m // bm, n // bn),
        in_specs=[pl.BlockSpec((bm, k), lambda i, j: (i, 0)),
                  pl.BlockSpec((k, bn), lambda i, j: (0, j)),
                  pl.BlockSpec((bm, bn), lambda i, j: (i, j)),
                  pl.BlockSpec((8, bn), lambda i, j: (jnp.maximum(i * rb - 1, 0), j)),
                  pl.BlockSpec((8, bn), lambda i, j: (jnp.minimum((i + 1) * rb, nrb - 1), j)),
                  pl.BlockSpec((3, bn), lambda i, j: (0, j)),
                  pl.BlockSpec((1, bn), lambda i, j: (0, j))],
        out_specs=pl.BlockSpec((bm, bn), lambda i, j: (i, j)),
        out_shape=jax.ShapeDtypeStruct((m, n), BF16),
        compiler_params=_cparams(("parallel", "parallel")),
        name="ffn_up",
    )(h, w_up, a, a, a, conv_w, conv_b)


def _conv_ffn(x2d, seq, norm_g, w_gate, w_up, conv_w, conv_b, w_down, layer):
    pad = D_FF_PAD - D_FF
    wg = _cast_pad(w_gate, layer, cols_out=D_FF_PAD)
    wu = _cast_pad(w_up, layer, cols_out=D_FF_PAD)
    wd = _cast_pad(w_down, layer, rows_out=D_FF_PAD)
    cw = jnp.pad(conv_w, ((0, 0), (0, pad)))
    cb = jnp.pad(conv_b, ((0, pad),)).reshape(1, D_FF_PAD)
    h = _rmsnorm(x2d, norm_g)
    a = _mm(h, wg, F32, name="ffn_gate")
    g = _ffn_up(h, wu, a, cw, cb, seq)
    return _mm_res_acc(g, wd, x2d, bk=D_FF_PAD // 4, name="ffn_down")


def _even_layer(x2d, batch, seq, e_norm, w_in, q_lora_g, kv_lora_g, w_uq, w_ukv, mla_qn, mla_kn,
                dil_qn, dil_kn, w_out, layer):
    c3 = MLA_Q_RANK + MLA_KV_RANK + MLA_ROPE
    w_lat = jnp.pad(w_in[:, :c3], ((0, 0), (0, LAT_PAD - c3))).astype(BF16)
    w_b = w_in[:, c3:].astype(BF16)
    hpad = MLA_HEAD_PAD - MLA_QK
    w_uq_p = jnp.pad(w_uq.reshape(MLA_Q_RANK, MLA_HEADS, MLA_QK), ((0, 0), (0, 0), (0, hpad)))
    w_uq_p = w_uq_p.reshape(MLA_Q_RANK, MLA_HEADS * MLA_HEAD_PAD).astype(BF16)
    gq_pad = jnp.pad(mla_qn * (MLA_QK ** -0.5 * LOG2E), (0, hpad)).reshape(1, MLA_HEAD_PAD)
    gk_pad = jnp.pad(mla_kn, (0, hpad)).reshape(1, MLA_HEAD_PAD)
    mla_tabs = _rope_tables(seq, MLA_ROPE, 128)
    dil_tabs = _rope_tables(seq, DIL_ROT, 128)

    h, cq, ckv, kpe = _lat_proj(x2d, e_norm, w_lat, q_lora_g, kv_lora_g)
    q = _mla_q_proj(cq, w_uq_p, gq_pad, mla_tabs, seq)
    k, v = _mla_kv_proj(ckv, w_ukv.astype(BF16), kpe, gk_pad, mla_tabs, seq)
    o_a = _mla_attention(q, k, v, batch, seq)

    outs, stats = [], []
    for g, (window, dil) in enumerate(DIL_CONFIGS):
        assert window // (2 * dil) == DIL_RADIUS
        qkv_g = _dil_proj(h, w_b, jnp.stack([dil_qn * (HEAD_DIM ** -0.5 * LOG2E), dil_kn]), dil_tabs,
                          batch, seq, g, dil)
        o, st = _dilated_group(qkv_g, g)
        outs.append(o)
        stats.append(st)
    o_b = _dil_merge(outs, stats, batch, seq)
    return _mm_res2(o_a, o_b, _cast_pad(w_out, layer), x2d, name="even_out_proj")


def _odd_layer(x2d, batch, seq, o_norm, w_qkv, qn, kn, rpb, w_out, layer):
    h = _rmsnorm(x2d, o_norm)
    qkv = _heads_proj(h, _cast_pad(w_qkv, layer),
                      jnp.stack([qn * (HEAD_DIM ** -0.5 * LOG2E), kn]), NA_HEADS * 128 // 1024,
                      name="na_qkv_proj")
    bias = _na_bias_tables(rpb, seq // GRID_W)
    o = _na_attention(qkv, bias, batch, seq)
    return _mm_res(o, _cast_pad(w_out, layer), x2d, name="odd_out_proj")


def kernel(x, e_norm, e_w_in, e_q_lora_norm, e_kv_lora_norm, e_w_uq, e_w_ukv, e_mla_q_norm,
           e_mla_k_norm, e_dil_q_norm, e_dil_k_norm, e_w_out, o_norm, o_w_qkv, o_q_norm, o_k_norm,
           o_rpb, o_w_out, f_norm, f_w_gate, f_w_up, f_conv_w, f_conv_b, f_w_down):
    batch, seq, d = x.shape
    x2d = x.reshape(batch * seq, d)
    depth = f_norm.shape[0]
    for layer in range(depth):
        i = layer // 2
        if layer % 2 == 0:
            x2d = _even_layer(x2d, batch, seq, e_norm[i], e_w_in[i], e_q_lora_norm[i],
                              e_kv_lora_norm[i], e_w_uq[i], e_w_ukv[i], e_mla_q_norm[i],
                              e_mla_k_norm[i], e_dil_q_norm[i], e_dil_k_norm[i], e_w_out, i)
        else:
            x2d = _odd_layer(x2d, batch, seq, o_norm[i], o_w_qkv, o_q_norm[i], o_k_norm[i],
                             o_rpb[i], o_w_out, i)
        x2d = _conv_ffn(x2d, seq, f_norm[layer], f_w_gate, f_w_up, f_conv_w[layer],
                        f_conv_b[layer], f_w_down, layer)
    return x2d.reshape(batch, seq, d)
```
